```python
import jax, jax.numpy as jnp
from jax import lax
import numpy as np

D_MODEL = 1024
BATCH = 8
SEQ = 2048
DEPTH = 1

MEM_LEN = 256
HEAD_DIM = 64
POOL_WINDOWS = (2, 4, 8, 16)
POOL_GROUPS = len(POOL_WINDOWS)
POOL_WIDTH = D_MODEL // 4
POOL_CH = POOL_WIDTH // POOL_GROUPS
SB_WIDTH = D_MODEL // 2
SB_HEADS = SB_WIDTH // HEAD_DIM
MEM_HEADS = 4
MEM_WIDTH = D_MODEL // 4
MEM_HEAD_DIM = MEM_WIDTH // MEM_HEADS
MIX_WIDTH = POOL_WIDTH + SB_WIDTH + MEM_WIDTH
IN_SPLITS = (POOL_WIDTH, POOL_WIDTH, SB_WIDTH, SB_WIDTH, SB_WIDTH, SB_WIDTH, MEM_WIDTH, MEM_WIDTH)
IN_WIDTH = sum(IN_SPLITS)
Q_BLOCK = 128
EPS = 1e-6

kernel_name = "hybrid_pool_stickbreak_memory_layer"


def rmsnorm(x, g):
    xf = x.astype(jnp.float32)
    y = xf * lax.rsqrt(jnp.mean(xf * xf, axis=-1, keepdims=True) + EPS)
    return (y * g.astype(jnp.float32)).astype(x.dtype)


def pool_mixer(u, pool_w, pool_scale):
    B, S, _ = u.shape
    uf = u.astype(jnp.float32).reshape(B, S, POOL_GROUPS, POOL_CH)
    csum = jnp.concatenate([jnp.zeros_like(uf[:, :1]), jnp.cumsum(uf, axis=1)], axis=1)
    t = jnp.arange(S)
    means = []
    for g, w in enumerate(POOL_WINDOWS):
        lo = jnp.maximum(t + 1 - w, 0)
        win_sum = csum[:, 1:, g] - csum[:, lo, g]
        count = (t + 1 - lo).astype(jnp.float32)
        means.append(win_sum / count[None, :, None])
    pooled = (jnp.stack(means, axis=2) - uf).astype(u.dtype)
    y = jnp.einsum('bsgc,gcd->bsgd', pooled, pool_w)
    return y.reshape(B, S, POOL_WIDTH) * pool_scale


def stick_breaking_attention(q, k, v):
    B, S, H, Dh = q.shape
    scale = Dh ** -0.5
    outs = []
    for i in range(S // Q_BLOCK):
        q0 = i * Q_BLOCK
        kv_len = q0 + Q_BLOCK
        qb = q[:, q0:kv_len]
        kb = k[:, :kv_len]
        vb = v[:, :kv_len]
        z = jnp.einsum('bqhd,bkhd->bhqk', qb, kb).astype(jnp.float32) * scale
        t_idx = q0 + jnp.arange(Q_BLOCK)
        s_idx = jnp.arange(kv_len)
        mask = s_idx[None, :] < t_idx[:, None]
        log_beta = jax.nn.log_sigmoid(z)
        log_1m_beta = jnp.where(mask, jax.nn.log_sigmoid(-z), 0.0)
        cum = jnp.cumsum(log_1m_beta, axis=-1)
        log_a = log_beta + cum[..., -1:] - cum
        a = jnp.where(mask, jnp.exp(log_a), 0.0)
        outs.append(jnp.einsum('bhqk,bkhd->bqhd', a.astype(v.dtype), vb))
    return jnp.concatenate(outs, axis=1)


def memory_attention(q, mem_k, mem_v, q_norm_g, k_norm_g):
    q = rmsnorm(q, q_norm_g)
    mem_k = rmsnorm(mem_k, k_norm_g)
    s = jnp.einsum('bqhd,bmhd->bhqm', q, mem_k).astype(jnp.float32) * (q.shape[-1] ** -0.5)
    p = jax.nn.softmax(s, axis=-1)
    return jnp.einsum('bhqm,bmhd->bqhd', p.astype(mem_v.dtype), mem_v)


def setup_inputs(seed: int = 0) -> dict:
    key = jax.random.key(seed)
    ks = jax.random.split(key, 12)
    f32 = jnp.float32
    x = jax.random.normal(ks[0], (BATCH, SEQ, D_MODEL), f32)
    mem = jax.random.normal(ks[1], (BATCH, MEM_LEN, D_MODEL), f32)
    norm_g = 1.0 + 0.02 * jax.random.normal(ks[2], (DEPTH, D_MODEL), f32)
    w_in = jax.random.normal(ks[3], (DEPTH, D_MODEL, IN_WIDTH), f32) * D_MODEL ** -0.5
    pool_w = jax.random.normal(ks[4], (DEPTH, POOL_GROUPS, POOL_CH, POOL_CH), f32) * POOL_CH ** -0.5
    pool_scale = 1.0 + 0.1 * jax.random.normal(ks[5], (DEPTH, POOL_WIDTH), f32)
    mem_norm_g = 1.0 + 0.02 * jax.random.normal(ks[6], (DEPTH, D_MODEL), f32)
    w_mem_kv = jax.random.normal(ks[7], (DEPTH, D_MODEL, 2 * MEM_WIDTH), f32) * D_MODEL ** -0.5
    q_norm_g = 1.0 + 0.02 * jax.random.normal(ks[8], (DEPTH, MEM_HEAD_DIM), f32)
    k_norm_g = 1.0 + 0.02 * jax.random.normal(ks[9], (DEPTH, MEM_HEAD_DIM), f32)
    w_out = jax.random.normal(ks[10], (DEPTH, MIX_WIDTH, D_MODEL), f32) * MIX_WIDTH ** -0.5
    return {"x": x, "mem": mem, "norm_g": norm_g, "w_in": w_in, "pool_w": pool_w,
            "pool_scale": pool_scale, "mem_norm_g": mem_norm_g, "w_mem_kv": w_mem_kv,
            "q_norm_g": q_norm_g, "k_norm_g": k_norm_g, "w_out": w_out}


def reference(x, mem, norm_g, w_in, pool_w, pool_scale, mem_norm_g, w_mem_kv, q_norm_g, k_norm_g, w_out):
    B, S, _ = x.shape
    M = mem.shape[1]
    split_points = list(np.cumsum(IN_SPLITS)[:-1])
    for l in range(DEPTH):
        h = rmsnorm(x, norm_g[l])
        proj = jnp.einsum('bsd,de->bse', h, w_in[l])
        (pool_v, pool_gate, sb_q, sb_k, sb_v, sb_gate,
         mem_q, mem_gate) = jnp.split(proj, split_points, axis=-1)

        y_pool = pool_mixer(pool_v, pool_w[l], pool_scale[l]) * jax.nn.silu(pool_gate)

        heads = lambda t: t.reshape(B, S, SB_HEADS, HEAD_DIM)
        y_sb = stick_breaking_attention(heads(sb_q), heads(sb_k), heads(sb_v)).reshape(B, S, SB_WIDTH)
        y_sb = y_sb * jax.nn.silu(sb_gate)

        mkv = jnp.einsum('bmd,de->bme', rmsnorm(mem, mem_norm_g[l]), w_mem_kv[l])
        mem_k, mem_v = jnp.split(mkv, 2, axis=-1)
        mem_k = mem_k.reshape(B, M, MEM_HEADS, MEM_HEAD_DIM)
        mem_v = mem_v.reshape(B, M, MEM_HEADS, MEM_HEAD_DIM)
        y_mem = memory_attention(mem_q.reshape(B, S, MEM_HEADS, MEM_HEAD_DIM), mem_k, mem_v,
                                 q_norm_g[l], k_norm_g[l]).reshape(B, S, MEM_WIDTH)
        y_mem = y_mem * jax.nn.silu(mem_gate)

        mixed = jnp.concatenate([y_pool, y_sb, y_mem], axis=-1)
        x = x + jnp.einsum('bse,ed->bsd', mixed, w_out[l])
    return x
```

```python
import functools
import math

import jax
import jax.numpy as jnp
from jax import lax
from jax.experimental import pallas as pl
from jax.experimental.pallas import tpu as pltpu

F32 = jnp.float32
BF16 = jnp.bfloat16

EPS = 1e-6
HEAD_DIM = 64
POOL_WINDOWS = (2, 4, 8, 16)
MAX_WINDOW = max(POOL_WINDOWS)
MEM_HEADS = 4
LANES = 128
HEADS_PER_LANE_TILE = LANES // HEAD_DIM

TM = 512
TQ = 256
VMEM_LIMIT = 56 * 1024 * 1024


def _silu(g):
    return g / (1.0 + jnp.exp(-g))


def _log2(v):
    return jnp.log(v) * (1.0 / math.log(2.0))


def _dot(a, b):
    return jnp.dot(a, b, preferred_element_type=F32)


def _dot_nt(a, b):
    return lax.dot_general(a, b, (((1,), (1,)), ((), ())), preferred_element_type=F32)


def _segment_mean_matrix(width, seg):
    r = lax.broadcasted_iota(jnp.int32, (width, width), 0) // seg
    c = lax.broadcasted_iota(jnp.int32, (width, width), 1) // seg
    return jnp.where(r == c, 1.0 / seg, 0.0).astype(BF16)


def _mem_kv_kernel(mem_ref, g_ref, w_ref, kg_ref, k_out, v_out):
    m = mem_ref[...]
    ms = jnp.mean(m * m, axis=-1, keepdims=True)
    h = (m * lax.rsqrt(ms + EPS) * g_ref[...]).astype(BF16)
    kv = _dot(h, w_ref[...])
    width = kv.shape[1] // 2
    k = kv[:, :width]
    v = kv[:, width:]
    seg = _segment_mean_matrix(width, width // MEM_HEADS)
    kms = _dot((k * k).astype(BF16), seg)
    k_out[...] = (k * lax.rsqrt(kms + EPS) * kg_ref[...]).astype(BF16)
    v_out[...] = v.astype(BF16)


def _mem_kv(mem2d, mem_norm_g, w_mem_kv, k_norm_g_tiled):
    rows, d = mem2d.shape
    width = w_mem_kv.shape[1] // 2
    tr = 512
    return pl.pallas_call(
        _mem_kv_kernel,
        out_shape=(jax.ShapeDtypeStruct((rows, width), BF16),
                   jax.ShapeDtypeStruct((rows, width), BF16)),
        grid=(rows // tr,),
        in_specs=[pl.BlockSpec((tr, d), lambda i: (i, 0)),
                  pl.BlockSpec((1, d), lambda i: (0, 0)),
                  pl.BlockSpec((d, 2 * width), lambda i: (0, 0)),
                  pl.BlockSpec((1, width), lambda i: (0, 0))],
        out_specs=(pl.BlockSpec((tr, width), lambda i: (i, 0)),
                   pl.BlockSpec((tr, width), lambda i: (i, 0))),
        compiler_params=pltpu.CompilerParams(dimension_semantics=("arbitrary",),
                                             vmem_limit_bytes=VMEM_LIMIT),
        name="mem_kv",
    )(mem2d, mem_norm_g, w_mem_kv, k_norm_g_tiled)


def _in_proj_kernel(seq_len, pool_w, sb_w, mem_w,
                    x_ref, g_ref, w_ref, pwbd_ref, pscale_ref, mk_ref, mv_ref, qg_ref,
                    q_out, k_out, v_out, gate_out, pm_out, pbuf):
    tm = x_ref.shape[0]
    row0 = pl.program_id(0) * tm
    t0 = row0 % seq_len

    x = x_ref[...]
    ms = jnp.mean(x * x, axis=-1, keepdims=True)
    h = (x * lax.rsqrt(ms + EPS) * g_ref[...]).astype(BF16)

    def proj(c0, width):
        return _dot(h, w_ref[:, c0:c0 + width])

    c = 0
    pool_v = proj(c, pool_w); c += pool_w
    pool_g = proj(c, pool_w); c += pool_w
    q = proj(c, sb_w); c += sb_w
    q_out[...] = (q * (math.log2(math.e) * HEAD_DIM ** -0.5)).astype(BF16)
    k_out[...] = proj(c, sb_w).astype(BF16); c += sb_w
    v_out[...] = proj(c, sb_w).astype(BF16); c += sb_w
    gate_out[...] = _silu(proj(c, sb_w)).astype(BF16); c += sb_w
    mem_q = proj(c, mem_w); c += mem_w
    mem_g = proj(c, mem_w)

    @pl.when(t0 == 0)
    def _():
        pbuf[0:MAX_WINDOW, :] = jnp.zeros((MAX_WINDOW, pool_w), F32)

    pbuf[MAX_WINDOW:MAX_WINDOW + tm, :] = pool_v
    lane = lax.broadcasted_iota(jnp.int32, (tm, pool_w), 1)
    group = lane // (pool_w // len(POOL_WINDOWS))
    t_seq = t0 + lax.broadcasted_iota(jnp.int32, (tm, pool_w), 0)
    win_sum = pool_v
    window = jnp.full((tm, pool_w), 1, jnp.int32)
    acc = pool_v
    shift = 1
    for g, w in enumerate(POOL_WINDOWS):
        while shift < w:
            acc = acc + pbuf[MAX_WINDOW - shift:MAX_WINDOW - shift + tm, :]
            shift += 1
        win_sum = jnp.where(group == g, acc, win_sum)
        window = jnp.where(group == g, w, window)
    count = jnp.minimum(t_seq + 1, window).astype(F32)
    pooled = (win_sum / count - pool_v).astype(BF16)
    pbuf[0:MAX_WINDOW, :] = pool_v[tm - MAX_WINDOW:, :]
    y_pool = _dot(pooled, pwbd_ref[...]) * pscale_ref[...] * _silu(pool_g)

    head_w = mem_w // MEM_HEADS
    seg = _segment_mean_matrix(mem_w, head_w)
    qms = _dot((mem_q * mem_q).astype(BF16), seg)
    qn = mem_q * lax.rsqrt(qms + EPS) * qg_ref[...]
    qn = qn * (head_w ** -0.5)
    mk = mk_ref[...]
    mv = mv_ref[...].astype(F32)
    lane_q = lax.broadcasted_iota(jnp.int32, (tm, mem_w), 1) // head_w
    lane_v = lax.broadcasted_iota(jnp.int32, mv.shape, 1) // head_w
    y_mem = jnp.zeros((tm, mem_w), F32)
    for hd in range(MEM_HEADS):
        qh = jnp.where(lane_q == hd, qn, 0.0).astype(BF16)
        s = _dot_nt(qh, mk)
        e = jnp.exp(s - jnp.max(s, axis=-1, keepdims=True))
        p = e * (1.0 / jnp.sum(e, axis=-1, keepdims=True))
        vh = jnp.where(lane_v == hd, mv, 0.0).astype(BF16)
        y_mem = y_mem + _dot(p.astype(BF16), vh)
    y_mem = y_mem * _silu(mem_g)

    pm_out[:, 0:pool_w] = y_pool.astype(BF16)
    pm_out[:, pool_w:pool_w + mem_w] = y_mem.astype(BF16)


def _in_proj(x2d, norm_g, w_in, pool_wbd, pool_scale, mem_k, mem_v, q_norm_g_tiled, seq_len, mem_len):
    rows, d = x2d.shape
    pool_w = pool_wbd.shape[0]
    mem_w = mem_k.shape[1]
    sb_w = (w_in.shape[1] - 2 * pool_w - 2 * mem_w) // 4
    assert seq_len % TM == 0 and rows % TM == 0
    tiles_per_seq = seq_len // TM
    row_spec = lambda width: pl.BlockSpec((TM, width), lambda i: (i, 0))
    full = lambda a: pl.BlockSpec(a.shape, lambda i: (0, 0))
    mem_spec = pl.BlockSpec((mem_len, mem_w), lambda i: (i // tiles_per_seq, 0))
    out_sd = lambda width: jax.ShapeDtypeStruct((rows, width), BF16)
    return pl.pallas_call(
        functools.partial(_in_proj_kernel, seq_len, pool_w, sb_w, mem_w),
        out_shape=(out_sd(sb_w), out_sd(sb_w), out_sd(sb_w), out_sd(sb_w), out_sd(pool_w + mem_w)),
        grid=(rows // TM,),
        in_specs=[row_spec(d), full(norm_g), full(w_in), full(pool_wbd), full(pool_scale),
                  mem_spec, mem_spec, full(q_norm_g_tiled)],
        out_specs=(row_spec(sb_w), row_spec(sb_w), row_spec(sb_w), row_spec(sb_w),
                   row_spec(pool_w + mem_w)),
        scratch_shapes=[pltpu.VMEM((MAX_WINDOW + TM, pool_w), F32)],
        compiler_params=pltpu.CompilerParams(dimension_semantics=("arbitrary",),
                                             vmem_limit_bytes=VMEM_LIMIT),
        name="in_proj",
    )(x2d, norm_g, w_in, pool_wbd, pool_scale, mem_k, mem_v, q_norm_g_tiled)


def _sb_out_kernel(n_heads,
                   q_ref, k_ref, v_ref, gate_ref, pm_ref, x_ref, tri_ref, wsb_ref, wpm_ref,
                   o_ref, qm_ref, vm_ref, acc_ref, carry_ref, mixed_ref):
    tq = q_ref.shape[0]
    n_pairs = n_heads // HEADS_PER_LANE_TILE
    qi = pl.program_id(1)
    half = lax.broadcasted_iota(jnp.int32, (1, LANES), 1) // HEAD_DIM

    @pl.when(qi == 0)
    def _():
        v = v_ref[...].astype(F32)
        lane_half = (lax.broadcasted_iota(jnp.int32, v.shape, 1) // HEAD_DIM) % HEADS_PER_LANE_TILE
        for hh in range(HEADS_PER_LANE_TILE):
            vm_ref[hh] = jnp.where(lane_half == hh, v, 0.0).astype(BF16)

    for p in range(n_pairs):
        qp = q_ref[:, p * LANES:(p + 1) * LANES].astype(F32)
        for hh in range(HEADS_PER_LANE_TILE):
            qm_ref[p * HEADS_PER_LANE_TILE + hh] = jnp.where(half == hh, qp, 0.0).astype(BF16)

    row = lax.broadcasted_iota(jnp.int32, (tq, tq), 0)
    col = lax.broadcasted_iota(jnp.int32, (tq, tq), 1)
    causal = col < row

    def block(kb, diagonal):
        k0 = pl.multiple_of(kb * tq, tq)
        for p in range(n_pairs):
            kp = k_ref[pl.ds(k0, tq), p * LANES:(p + 1) * LANES]
            out = None
            for hh in range(HEADS_PER_LANE_TILE):
                hd = p * HEADS_PER_LANE_TILE + hh
                z2 = _dot_nt(qm_ref[hd], kp)
                sp = jnp.maximum(z2, 0.0) + _log2(1.0 + jnp.exp2(-jnp.abs(z2)))
                if diagonal:
                    sp = jnp.where(causal, sp, 0.0)
                suffix = _dot(sp.astype(BF16), tri_ref[...])
                log2_a = z2 + suffix
                if not diagonal:
                    log2_a = log2_a + carry_ref[hd]
                a = jnp.exp2(log2_a)
                if diagonal:
                    a = jnp.where(causal, a, 0.0)
                    carry_ref[hd] = suffix[:, 0:1]
                else:
                    carry_ref[hd] = carry_ref[hd] + suffix[:, 0:1]
                pv = _dot(a.astype(BF16), vm_ref[hh, pl.ds(k0, tq), p * LANES:(p + 1) * LANES])
                out = pv if out is None else out + pv
            if diagonal:
                acc_ref[p] = out
            else:
                acc_ref[p] = acc_ref[p] + out

    block(qi, True)

    def body(j, _):
        block(qi - 1 - j, False)
        return 0

    lax.fori_loop(0, qi, body, 0)

    for p in range(n_pairs):
        sl = slice(p * LANES, (p + 1) * LANES)
        mixed_ref[:, sl] = (acc_ref[p] * gate_ref[:, sl].astype(F32)).astype(BF16)
    o_ref[...] = x_ref[...] + _dot(mixed_ref[...], wsb_ref[...]) + _dot(pm_ref[...], wpm_ref[...])


def _sb_out(q, k, v, gate, pm, x2d, tri, w_out_sb, w_out_pm, batch, seq_len):
    rows, sb_w = q.shape
    d = x2d.shape[1]
    n_heads = sb_w // HEAD_DIM
    nq = seq_len // TQ
    k3 = k.reshape(batch, seq_len, sb_w)
    v3 = v.reshape(batch, seq_len, sb_w)
    row_spec = lambda width: pl.BlockSpec((TQ, width), lambda b, i: (b * nq + i, 0))
    seq_spec = pl.BlockSpec((None, seq_len, sb_w), lambda b, i: (b, 0, 0))
    full = lambda a: pl.BlockSpec(a.shape, lambda b, i: (0, 0))
    return pl.pallas_call(
        functools.partial(_sb_out_kernel, n_heads),
        out_shape=jax.ShapeDtypeStruct((rows, d), F32),
        grid=(batch, nq),
        in_specs=[row_spec(sb_w), seq_spec, seq_spec, row_spec(sb_w), row_spec(pm.shape[1]),
                  row_spec(d), full(tri), full(w_out_sb), full(w_out_pm)],
        out_specs=row_spec(d),
        scratch_shapes=[pltpu.VMEM((n_heads, TQ, LANES), BF16),
                        pltpu.VMEM((HEADS_PER_LANE_TILE, seq_len, sb_w), BF16),
                        pltpu.VMEM((n_heads // HEADS_PER_LANE_TILE, TQ, LANES), F32),
                        pltpu.VMEM((n_heads, TQ, 1), F32),
                        pltpu.VMEM((TQ, sb_w), BF16)],
        compiler_params=pltpu.CompilerParams(dimension_semantics=("arbitrary", "arbitrary"),
                                             vmem_limit_bytes=VMEM_LIMIT),
        name="sb_out",
    )(q, k3, v3, gate, pm, x2d, tri, w_out_sb, w_out_pm)


def _block_diag(w):
    g, c, d = w.shape
    out = jnp.zeros((g * c, g * d), w.dtype)
    for i in range(g):
        out = out.at[i * c:(i + 1) * c, i * d:(i + 1) * d].set(w[i])
    return out


def kernel(x, mem, norm_g, w_in, pool_w, pool_scale, mem_norm_g, w_mem_kv, q_norm_g, k_norm_g, w_out):
    batch, seq_len, d = x.shape
    mem_len = mem.shape[1]
    depth = norm_g.shape[0]
    pool_width = pool_scale.shape[1]
    mem_width = w_mem_kv.shape[2] // 2
    assert seq_len % TQ == 0
    tri = jnp.where(lax.broadcasted_iota(jnp.int32, (TQ, TQ), 0) >= lax.broadcasted_iota(jnp.int32, (TQ, TQ), 1),
                    -1.0, 0.0).astype(BF16)
    x2d = x.reshape(batch * seq_len, d)
    for l in range(depth):
        mem_k, mem_v = _mem_kv(mem.reshape(batch * mem_len, d), mem_norm_g[l][None],
                               w_mem_kv[l].astype(BF16), jnp.tile(k_norm_g[l], MEM_HEADS)[None])
        q, k, v, gate, pm = _in_proj(x2d, norm_g[l][None], w_in[l].astype(BF16),
                                     _block_diag(pool_w[l]).astype(BF16), pool_scale[l][None],
                                     mem_k, mem_v, jnp.tile(q_norm_g[l], MEM_HEADS)[None],
                                     seq_len, mem_len)
        w_o = w_out[l].astype(BF16)
        w_out_sb = w_o[pool_width:w_o.shape[0] - mem_width]
        w_out_pm = jnp.concatenate([w_o[:pool_width], w_o[w_o.shape[0] - mem_width:]], axis=0)
        x2d = _sb_out(q, k, v, gate, pm, x2d, tri, w_out_sb, w_out_pm, batch, seq_len)
    return x2d.reshape(batch, seq_len, d)
```

```python
import functools
import math

import jax
import jax.numpy as jnp
from jax import lax
from jax.experimental import pallas as pl
from jax.experimental.pallas import tpu as pltpu

F32 = jnp.float32
BF16 = jnp.bfloat16

EPS = 1e-6
HEAD_DIM = 64
POOL_WINDOWS = (2, 4, 8, 16)
MAX_WINDOW = max(POOL_WINDOWS)
MEM_HEADS = 4
LANES = 128
HEADS_PER_LANE_TILE = LANES // HEAD_DIM
PV_GROUP = HEADS_PER_LANE_TILE
PV_WIDTH = PV_GROUP * HEAD_DIM

TM = 512
TQ = 256
VMEM_LIMIT = 56 * 1024 * 1024


def _silu(g):
    return g / (1.0 + jnp.exp(-g))


def _log2(v):
    return jnp.log(v) * (1.0 / math.log(2.0))


def _dot(a, b):
    return jnp.dot(a, b, preferred_element_type=F32)


def _dot_nt(a, b):
    return lax.dot_general(a, b, (((1,), (1,)), ((), ())), preferred_element_type=F32)


def _segment_mean_matrix(width, seg):
    r = lax.broadcasted_iota(jnp.int32, (width, width), 0) // seg
    c = lax.broadcasted_iota(jnp.int32, (width, width), 1) // seg
    return jnp.where(r == c, 1.0 / seg, 0.0).astype(BF16)


def _mem_kv_kernel(mem_ref, g_ref, w_ref, kg_ref, k_out, v_out):
    m = mem_ref[...]
    ms = jnp.mean(m * m, axis=-1, keepdims=True)
    h = (m * lax.rsqrt(ms + EPS) * g_ref[...]).astype(BF16)
    kv = _dot(h, w_ref[...])
    width = kv.shape[1] // 2
    k = kv[:, :width]
    v = kv[:, width:]
    seg = _segment_mean_matrix(width, width // MEM_HEADS)
    kms = _dot((k * k).astype(BF16), seg)
    k_out[...] = (k * lax.rsqrt(kms + EPS) * kg_ref[...]).astype(BF16)
    v_out[...] = v.astype(BF16)


def _mem_kv(mem2d, mem_norm_g, w_mem_kv, k_norm_g_tiled):
    rows, d = mem2d.shape
    width = w_mem_kv.shape[1] // 2
    tr = 512
    return pl.pallas_call(
        _mem_kv_kernel,
        out_shape=(jax.ShapeDtypeStruct((rows, width), BF16),
                   jax.ShapeDtypeStruct((rows, width), BF16)),
        grid=(rows // tr,),
        in_specs=[pl.BlockSpec((tr, d), lambda i: (i, 0)),
                  pl.BlockSpec((1, d), lambda i: (0, 0)),
                  pl.BlockSpec((d, 2 * width), lambda i: (0, 0)),
                  pl.BlockSpec((1, width), lambda i: (0, 0))],
        out_specs=(pl.BlockSpec((tr, width), lambda i: (i, 0)),
                   pl.BlockSpec((tr, width), lambda i: (i, 0))),
        compiler_params=pltpu.CompilerParams(dimension_semantics=("arbitrary",),
                                             vmem_limit_bytes=VMEM_LIMIT),
        name="mem_kv",
    )(mem2d, mem_norm_g, w_mem_kv, k_norm_g_tiled)


def _in_proj_kernel(seq_len, pool_w, sb_w, mem_w,
                    x_ref, g_ref, w_ref, pwbd_ref, pscale_ref, mk_ref, mv_ref, qg_ref,
                    q_out, k_out, v_out, gate_out, pm_out, pbuf):
    tm = x_ref.shape[0]
    row0 = pl.program_id(0) * tm
    t0 = row0 % seq_len

    x = x_ref[...]
    ms = jnp.mean(x * x, axis=-1, keepdims=True)
    h = (x * lax.rsqrt(ms + EPS) * g_ref[...]).astype(BF16)

    def proj(c0, width):
        return _dot(h, w_ref[:, c0:c0 + width])

    c = 0
    pool_v = proj(c, pool_w); c += pool_w
    pool_g = proj(c, pool_w); c += pool_w
    q = proj(c, sb_w); c += sb_w
    q_out[...] = (q * (math.log2(math.e) * HEAD_DIM ** -0.5)).astype(BF16)
    k_out[...] = proj(c, sb_w).astype(BF16); c += sb_w
    v_out[...] = proj(c, sb_w).astype(BF16); c += sb_w
    gate_out[...] = _silu(proj(c, sb_w)).astype(BF16); c += sb_w
    mem_q = proj(c, mem_w); c += mem_w
    mem_g = proj(c, mem_w)

    @pl.when(t0 == 0)
    def _():
        pbuf[0:MAX_WINDOW, :] = jnp.zeros((MAX_WINDOW, pool_w), F32)

    pbuf[MAX_WINDOW:MAX_WINDOW + tm, :] = pool_v
    lane = lax.broadcasted_iota(jnp.int32, (tm, pool_w), 1)
    group = lane // (pool_w // len(POOL_WINDOWS))
    t_seq = t0 + lax.broadcasted_iota(jnp.int32, (tm, pool_w), 0)
    win_sum = pool_v
    window = jnp.full((tm, pool_w), 1, jnp.int32)
    acc = pool_v
    shift = 1
    for g, w in enumerate(POOL_WINDOWS):
        while shift < w:
            acc = acc + pbuf[MAX_WINDOW - shift:MAX_WINDOW - shift + tm, :]
            shift += 1
        win_sum = jnp.where(group == g, acc, win_sum)
        window = jnp.where(group == g, w, window)
    count = jnp.minimum(t_seq + 1, window).astype(F32)
    pooled = (win_sum / count - pool_v).astype(BF16)
    pbuf[0:MAX_WINDOW, :] = pool_v[tm - MAX_WINDOW:, :]
    y_pool = _dot(pooled, pwbd_ref[...]) * pscale_ref[...] * _silu(pool_g)

    head_w = mem_w // MEM_HEADS
    seg = _segment_mean_matrix(mem_w, head_w)
    qms = _dot((mem_q * mem_q).astype(BF16), seg)
    qn = mem_q * lax.rsqrt(qms + EPS) * qg_ref[...]
    qn = qn * (head_w ** -0.5)
    mk = mk_ref[...]
    mv = mv_ref[...].astype(F32)
    lane_q = lax.broadcasted_iota(jnp.int32, (tm, mem_w), 1) // head_w
    lane_v = lax.broadcasted_iota(jnp.int32, mv.shape, 1) // head_w
    y_mem = jnp.zeros((tm, mem_w), F32)
    for hd in range(MEM_HEADS):
        qh = jnp.where(lane_q == hd, qn, 0.0).astype(BF16)
        s = _dot_nt(qh, mk)
        e = jnp.exp(s - jnp.max(s, axis=-1, keepdims=True))
        p = e * (1.0 / jnp.sum(e, axis=-1, keepdims=True))
        vh = jnp.where(lane_v == hd, mv, 0.0).astype(BF16)
        y_mem = y_mem + _dot(p.astype(BF16), vh)
    y_mem = y_mem * _silu(mem_g)

    pm_out[:, 0:pool_w] = y_pool.astype(BF16)
    pm_out[:, pool_w:pool_w + mem_w] = y_mem.astype(BF16)


def _in_proj(x2d, norm_g, w_in, pool_wbd, pool_scale, mem_k, mem_v, q_norm_g_tiled, seq_len, mem_len):
    rows, d = x2d.shape
    pool_w = pool_wbd.shape[0]
    mem_w = mem_k.shape[1]
    sb_w = (w_in.shape[1] - 2 * pool_w - 2 * mem_w) // 4
    assert seq_len % TM == 0 and rows % TM == 0
    tiles_per_seq = seq_len // TM
    row_spec = lambda width: pl.BlockSpec((TM, width), lambda i: (i, 0))
    full = lambda a: pl.BlockSpec(a.shape, lambda i: (0, 0))
    mem_spec = pl.BlockSpec((mem_len, mem_w), lambda i: (i // tiles_per_seq, 0))
    out_sd = lambda width: jax.ShapeDtypeStruct((rows, width), BF16)
    return pl.pallas_call(
        functools.partial(_in_proj_kernel, seq_len, pool_w, sb_w, mem_w),
        out_shape=(out_sd(sb_w), out_sd(sb_w), out_sd(sb_w), out_sd(sb_w), out_sd(pool_w + mem_w)),
        grid=(rows // TM,),
        in_specs=[row_spec(d), full(norm_g), full(w_in), full(pool_wbd), full(pool_scale),
                  mem_spec, mem_spec, full(q_norm_g_tiled)],
        out_specs=(row_spec(sb_w), row_spec(sb_w), row_spec(sb_w), row_spec(sb_w),
                   row_spec(pool_w + mem_w)),
        scratch_shapes=[pltpu.VMEM((MAX_WINDOW + TM, pool_w), F32)],
        compiler_params=pltpu.CompilerParams(dimension_semantics=("arbitrary",),
                                             vmem_limit_bytes=VMEM_LIMIT),
        name="in_proj",
    )(x2d, norm_g, w_in, pool_wbd, pool_scale, mem_k, mem_v, q_norm_g_tiled)


def _sb_out_kernel(n_heads,
                   q_ref, k_ref, v_ref, gate_ref, pm_ref, x_ref, tri_ref, wsb_ref, wpm_ref,
                   o_ref, qm_ref, vm_ref, acc_ref, carry_ref, mixed_ref):
    tq = q_ref.shape[0]
    n_pairs = n_heads // HEADS_PER_LANE_TILE
    qi = pl.program_id(1)
    half = lax.broadcasted_iota(jnp.int32, (1, LANES), 1) // HEAD_DIM

    @pl.when(qi == 0)
    def _():
        lane_head = (lax.broadcasted_iota(jnp.int32, (tq, v_ref.shape[1]), 1) // HEAD_DIM) % PV_GROUP
        for kb in range(v_ref.shape[0] // tq):
            v = v_ref[kb * tq:(kb + 1) * tq, :].astype(F32)
            for g in range(PV_GROUP):
                vm_ref[kb, g * tq:(g + 1) * tq, :] = jnp.where(lane_head == g, v, 0.0).astype(BF16)

    for p in range(n_pairs):
        qp = q_ref[:, p * LANES:(p + 1) * LANES].astype(F32)
        for hh in range(HEADS_PER_LANE_TILE):
            qm_ref[p * HEADS_PER_LANE_TILE + hh] = jnp.where(half == hh, qp, 0.0).astype(BF16)

    row = lax.broadcasted_iota(jnp.int32, (tq, tq), 0)
    col = lax.broadcasted_iota(jnp.int32, (tq, tq), 1)
    causal = col < row

    def block(kb, diagonal):
        k0 = pl.multiple_of(kb * tq, tq)
        z2, suffix, a = {}, {}, {}

        def logits(hd):
            p = hd // HEADS_PER_LANE_TILE
            z2[hd] = _dot_nt(qm_ref[hd], k_ref[pl.ds(k0, tq), p * LANES:(p + 1) * LANES])

        def suffix_sum(hd):
            z = z2[hd]
            sp = jnp.maximum(z, 0.0) + _log2(1.0 + jnp.exp2(-jnp.abs(z)))
            if diagonal:
                sp = jnp.where(causal, sp, 0.0)
            suffix[hd] = _dot(sp.astype(BF16), tri_ref[...])

        def weights(hd):
            log2_a = z2.pop(hd) + suffix[hd]
            if not diagonal:
                log2_a = log2_a + carry_ref[hd]
            w = jnp.exp2(log2_a)
            if diagonal:
                w = jnp.where(causal, w, 0.0)
                carry_ref[hd] = suffix.pop(hd)[:, 0:1]
            else:
                carry_ref[hd] = carry_ref[hd] + suffix.pop(hd)[:, 0:1]
            a[hd] = w.astype(BF16)

        def weighted_values(grp):
            heads = [grp * PV_GROUP + g for g in range(PV_GROUP)]
            a_cat = jnp.concatenate([a.pop(hd) for hd in heads], axis=1)
            out = _dot(a_cat, vm_ref[kb, :, grp * PV_WIDTH:(grp + 1) * PV_WIDTH])
            if diagonal:
                acc_ref[grp] = out
            else:
                acc_ref[grp] = acc_ref[grp] + out

        for step in range(n_heads + 2):
            if step < n_heads:
                logits(step)
            if 0 <= step - 1 < n_heads:
                suffix_sum(step - 1)
            if 0 <= step - 2 < n_heads:
                weights(step - 2)
                if (step - 2) % PV_GROUP == PV_GROUP - 1:
                    weighted_values((step - 2) // PV_GROUP)

    block(qi, True)

    def body(j, _):
        block(qi - 1 - j, False)
        return 0

    lax.fori_loop(0, qi, body, 0)

    for grp in range(n_heads // PV_GROUP):
        sl = slice(grp * PV_WIDTH, (grp + 1) * PV_WIDTH)
        mixed_ref[:, sl] = (acc_ref[grp] * gate_ref[:, sl].astype(F32)).astype(BF16)
    o_ref[...] = x_ref[...] + _dot(mixed_ref[...], wsb_ref[...]) + _dot(pm_ref[...], wpm_ref[...])


def _sb_out(q, k, v, gate, pm, x2d, tri, w_out_sb, w_out_pm, batch, seq_len):
    rows, sb_w = q.shape
    d = x2d.shape[1]
    n_heads = sb_w // HEAD_DIM
    nq = seq_len // TQ
    k3 = k.reshape(batch, seq_len, sb_w)
    v3 = v.reshape(batch, seq_len, sb_w)
    row_spec = lambda width: pl.BlockSpec((TQ, width), lambda b, i: (b * nq + i, 0))
    seq_spec = pl.BlockSpec((None, seq_len, sb_w), lambda b, i: (b, 0, 0))
    full = lambda a: pl.BlockSpec(a.shape, lambda b, i: (0, 0))
    return pl.pallas_call(
        functools.partial(_sb_out_kernel, n_heads),
        out_shape=jax.ShapeDtypeStruct((rows, d), F32),
        grid=(batch, nq),
        in_specs=[row_spec(sb_w), seq_spec, seq_spec, row_spec(sb_w), row_spec(pm.shape[1]),
                  row_spec(d), full(tri), full(w_out_sb), full(w_out_pm)],
        out_specs=row_spec(d),
        scratch_shapes=[pltpu.VMEM((n_heads, TQ, LANES), BF16),
                        pltpu.VMEM((nq, PV_GROUP * TQ, sb_w), BF16),
                        pltpu.VMEM((n_heads // PV_GROUP, TQ, PV_WIDTH), F32),
                        pltpu.VMEM((n_heads, TQ, 1), F32),
                        pltpu.VMEM((TQ, sb_w), BF16)],
        compiler_params=pltpu.CompilerParams(dimension_semantics=("arbitrary", "arbitrary"),
                                             vmem_limit_bytes=VMEM_LIMIT),
        name="sb_out",
    )(q, k3, v3, gate, pm, x2d, tri, w_out_sb, w_out_pm)


def _block_diag(w):
    g, c, d = w.shape
    out = jnp.zeros((g * c, g * d), w.dtype)
    for i in range(g):
        out = out.at[i * c:(i + 1) * c, i * d:(i + 1) * d].set(w[i])
    return out


def kernel(x, mem, norm_g, w_in, pool_w, pool_scale, mem_norm_g, w_mem_kv, q_norm_g, k_norm_g, w_out):
    batch, seq_len, d = x.shape
    mem_len = mem.shape[1]
    depth = norm_g.shape[0]
    pool_width = pool_scale.shape[1]
    mem_width = w_mem_kv.shape[2] // 2
    assert seq_len % TQ == 0
    tri = jnp.where(lax.broadcasted_iota(jnp.int32, (TQ, TQ), 0) >= lax.broadcasted_iota(jnp.int32, (TQ, TQ), 1),
                    -1.0, 0.0).astype(BF16)
    x2d = x.reshape(batch * seq_len, d)
    for l in range(depth):
        mem_k, mem_v = _mem_kv(mem.reshape(batch * mem_len, d), mem_norm_g[l][None],
                               w_mem_kv[l].astype(BF16), jnp.tile(k_norm_g[l], MEM_HEADS)[None])
        q, k, v, gate, pm = _in_proj(x2d, norm_g[l][None], w_in[l].astype(BF16),
                                     _block_diag(pool_w[l]).astype(BF16), pool_scale[l][None],
                                     mem_k, mem_v, jnp.tile(q_norm_g[l], MEM_HEADS)[None],
                                     seq_len, mem_len)
        w_o = w_out[l].astype(BF16)
        w_out_sb = w_o[pool_width:w_o.shape[0] - mem_width]
        w_out_pm = jnp.concatenate([w_o[:pool_width], w_o[w_o.shape[0] - mem_width:]], axis=0)
        x2d = _sb_out(q, k, v, gate, pm, x2d, tri, w_out_sb, w_out_pm, batch, seq_len)
    return x2d.reshape(batch, seq_len, d)
```

```python
import functools

import jax
import jax.numpy as jnp
from jax import lax
from jax.experimental import pallas as pl
from jax.experimental.pallas import tpu as pltpu

F32 = jnp.float32
BF16 = jnp.bfloat16

EPS = 1e-6
HEAD_DIM = 64
POOL_WINDOWS = (2, 4, 8, 16)
MAX_WINDOW = max(POOL_WINDOWS)
MEM_HEADS = 4
LANES = 128
HEADS_PER_LANE_TILE = LANES // HEAD_DIM
PV_GROUP = HEADS_PER_LANE_TILE
PV_WIDTH = PV_GROUP * HEAD_DIM

TM = 1024
TM_SUB = 256
TQ = 256
Q_TILES = 2
VMEM_LIMIT = 56 * 1024 * 1024


def _silu(g):
    return g / (1.0 + jnp.exp(-g))


def _dot(a, b):
    return jnp.dot(a, b, preferred_element_type=F32)


def _dot_nt(a, b):
    return lax.dot_general(a, b, (((1,), (1,)), ((), ())), preferred_element_type=F32)


def _segment_mean_matrix(width, seg):
    r = lax.broadcasted_iota(jnp.int32, (width, width), 0) // seg
    c = lax.broadcasted_iota(jnp.int32, (width, width), 1) // seg
    return jnp.where(r == c, 1.0 / seg, 0.0).astype(BF16)


def _mem_kv_kernel(mem_ref, g_ref, w_ref, kg_ref, k_out, v_out):
    m = mem_ref[...]
    ms = jnp.mean(m * m, axis=-1, keepdims=True)
    h = (m * lax.rsqrt(ms + EPS) * g_ref[...]).astype(BF16)
    kv = _dot(h, w_ref[...])
    width = kv.shape[1] // 2
    k = kv[:, :width]
    v = kv[:, width:]
    seg = _segment_mean_matrix(width, width // MEM_HEADS)
    kms = _dot((k * k).astype(BF16), seg)
    k_out[...] = (k * lax.rsqrt(kms + EPS) * kg_ref[...]).astype(BF16)
    v_out[...] = v.astype(BF16)


def _mem_kv(mem2d, mem_norm_g, w_mem_kv, k_norm_g_tiled):
    rows, d = mem2d.shape
    width = w_mem_kv.shape[1] // 2
    tr = 512
    return pl.pallas_call(
        _mem_kv_kernel,
        out_shape=(jax.ShapeDtypeStruct((rows, width), BF16),
                   jax.ShapeDtypeStruct((rows, width), BF16)),
        grid=(rows // tr,),
        in_specs=[pl.BlockSpec((tr, d), lambda i: (i, 0)),
                  pl.BlockSpec((1, d), lambda i: (0, 0)),
                  pl.BlockSpec((d, 2 * width), lambda i: (0, 0)),
                  pl.BlockSpec((1, width), lambda i: (0, 0))],
        out_specs=(pl.BlockSpec((tr, width), lambda i: (i, 0)),
                   pl.BlockSpec((tr, width), lambda i: (i, 0))),
        compiler_params=pltpu.CompilerParams(dimension_semantics=("arbitrary",),
                                             vmem_limit_bytes=VMEM_LIMIT),
        name="mem_kv",
    )(mem2d, mem_norm_g, w_mem_kv, k_norm_g_tiled)


def _in_proj_kernel(seq_len, pool_w, sb_w, mem_w,
                    x_ref, g_ref, w_ref, pwbd_ref, pscale_ref, mk_ref, mv_ref, qg_ref,
                    q_out, k_out, v_out, gate_out, pm_out, pbuf, halo_ref):
    tm = x_ref.shape[0]
    n_sub = pbuf.shape[0]
    rows = tm // n_sub
    t0 = (pl.program_id(0) * tm) % seq_len
    head_w = mem_w // MEM_HEADS

    @pl.when(t0 == 0)
    def _():
        halo_ref[...] = jnp.zeros(halo_ref.shape, F32)

    lane = lax.broadcasted_iota(jnp.int32, (rows, pool_w), 1)
    group = lane // (pool_w // len(POOL_WINDOWS))
    row_iota = lax.broadcasted_iota(jnp.int32, (rows, pool_w), 0)
    seg = _segment_mean_matrix(mem_w, head_w)
    lane_q = lax.broadcasted_iota(jnp.int32, (rows, mem_w), 1) // head_w
    mv = mv_ref[...].astype(F32)
    lane_v = lax.broadcasted_iota(jnp.int32, mv.shape, 1) // head_w
    mv_heads = [jnp.where(lane_v == hd, mv, 0.0).astype(BF16) for hd in range(MEM_HEADS)]

    h, kept = {}, {}

    def normalize(s):
        x = x_ref[s * rows:(s + 1) * rows, :]
        ms = jnp.mean(x * x, axis=-1, keepdims=True)
        h[s] = (x * lax.rsqrt(ms + EPS) * g_ref[...]).astype(BF16)

    def project(s):
        hs = h.pop(s)
        rs = slice(s * rows, (s + 1) * rows)

        def proj(c0, width):
            return _dot(hs, w_ref[:, c0:c0 + width])

        c = 0
        pool_v = proj(c, pool_w); c += pool_w
        pool_g = proj(c, pool_w); c += pool_w
        q = proj(c, sb_w); c += sb_w
        q_out[rs, :] = (q * HEAD_DIM ** -0.5).astype(BF16)
        k_out[rs, :] = proj(c, sb_w).astype(BF16); c += sb_w
        v_out[rs, :] = proj(c, sb_w).astype(BF16); c += sb_w
        gate_out[rs, :] = _silu(proj(c, sb_w)).astype(BF16); c += sb_w
        mem_q = proj(c, mem_w); c += mem_w
        mem_g = proj(c, mem_w)
        kept[s] = (pool_v, pool_g, mem_q, mem_g)

    def mix(s):
        pool_v, pool_g, mem_q, mem_g = kept.pop(s)
        rs = slice(s * rows, (s + 1) * rows)

        pbuf[s, 0:MAX_WINDOW, :] = halo_ref[...]
        pbuf[s, MAX_WINDOW:MAX_WINDOW + rows, :] = pool_v
        halo_ref[...] = pool_v[rows - MAX_WINDOW:, :]
        t_seq = t0 + s * rows + row_iota
        win_sum = pool_v
        window = jnp.full((rows, pool_w), 1, jnp.int32)
        acc = pool_v
        shift = 1
        for g, w in enumerate(POOL_WINDOWS):
            while shift < w:
                acc = acc + pbuf[s, MAX_WINDOW - shift:MAX_WINDOW - shift + rows, :]
                shift += 1
            win_sum = jnp.where(group == g, acc, win_sum)
            window = jnp.where(group == g, w, window)
        count = jnp.minimum(t_seq + 1, window).astype(F32)
        pooled = (win_sum / count - pool_v).astype(BF16)
        y_pool = _dot(pooled, pwbd_ref[...]) * pscale_ref[...] * _silu(pool_g)

        qms = _dot((mem_q * mem_q).astype(BF16), seg)
        qn = mem_q * lax.rsqrt(qms + EPS) * qg_ref[...]
        qn = qn * (head_w ** -0.5)
        y_mem = jnp.zeros((rows, mem_w), F32)
        for hd in range(MEM_HEADS):
            qh = jnp.where(lane_q == hd, qn, 0.0).astype(BF16)
            sc = _dot_nt(qh, mk_ref[...])
            e = jnp.exp(sc - jnp.max(sc, axis=-1, keepdims=True))
            p = e * (1.0 / jnp.sum(e, axis=-1, keepdims=True))
            y_mem = y_mem + _dot(p.astype(BF16), mv_heads[hd])
        y_mem = y_mem * _silu(mem_g)

        pm_out[rs, 0:pool_w] = y_pool.astype(BF16)
        pm_out[rs, pool_w:pool_w + mem_w] = y_mem.astype(BF16)

    for step in range(n_sub + 2):
        if step < n_sub:
            normalize(step)
        if 0 <= step - 1 < n_sub:
            project(step - 1)
        if 0 <= step - 2 < n_sub:
            mix(step - 2)


def _in_proj(x2d, norm_g, w_in, pool_wbd, pool_scale, mem_k, mem_v, q_norm_g_tiled, seq_len, mem_len):
    rows, d = x2d.shape
    pool_w = pool_wbd.shape[0]
    mem_w = mem_k.shape[1]
    sb_w = (w_in.shape[1] - 2 * pool_w - 2 * mem_w) // 4
    assert seq_len % TM == 0 and rows % TM == 0
    tiles_per_seq = seq_len // TM
    row_spec = lambda width: pl.BlockSpec((TM, width), lambda i: (i, 0))
    full = lambda a: pl.BlockSpec(a.shape, lambda i: (0, 0))
    mem_spec = pl.BlockSpec((mem_len, mem_w), lambda i: (i // tiles_per_seq, 0))
    out_sd = lambda width: jax.ShapeDtypeStruct((rows, width), BF16)
    return pl.pallas_call(
        functools.partial(_in_proj_kernel, seq_len, pool_w, sb_w, mem_w),
        out_shape=(out_sd(sb_w), out_sd(sb_w), out_sd(sb_w), out_sd(sb_w), out_sd(pool_w + mem_w)),
        grid=(rows // TM,),
        in_specs=[row_spec(d), full(norm_g), full(w_in), full(pool_wbd), full(pool_scale),
                  mem_spec, mem_spec, full(q_norm_g_tiled)],
        out_specs=(row_spec(sb_w), row_spec(sb_w), row_spec(sb_w), row_spec(sb_w),
                   row_spec(pool_w + mem_w)),
        scratch_shapes=[pltpu.VMEM((TM // TM_SUB, MAX_WINDOW + TM_SUB, pool_w), F32),
                        pltpu.VMEM((MAX_WINDOW, pool_w), F32)],
        compiler_params=pltpu.CompilerParams(dimension_semantics=("arbitrary",),
                                             vmem_limit_bytes=VMEM_LIMIT),
        name="in_proj",
    )(x2d, norm_g, w_in, pool_wbd, pool_scale, mem_k, mem_v, q_norm_g_tiled)


def _sb_out_kernel(n_heads,
                   q_ref, k_ref, v_ref, gate_ref, pm_ref, x_ref, tri_ref, wsb_ref, wpm_ref,
                   o_ref, qm_ref, vm_ref, acc_ref, carry_ref, mixed_ref):
    tq = tri_ref.shape[0]
    n_tiles = q_ref.shape[0] // tq
    n_pairs = n_heads // HEADS_PER_LANE_TILE
    first_block = pl.program_id(1) * n_tiles
    half = lax.broadcasted_iota(jnp.int32, (1, LANES), 1) // HEAD_DIM

    @pl.when(pl.program_id(1) == 0)
    def _():
        lane_head = (lax.broadcasted_iota(jnp.int32, (tq, v_ref.shape[1]), 1) // HEAD_DIM) % PV_GROUP
        for kb in range(v_ref.shape[0] // tq):
            v = v_ref[kb * tq:(kb + 1) * tq, :].astype(F32)
            for g in range(PV_GROUP):
                vm_ref[kb, g * tq:(g + 1) * tq, :] = jnp.where(lane_head == g, v, 0.0).astype(BF16)

    for t in range(n_tiles):
        for p in range(n_pairs):
            qp = q_ref[t * tq:(t + 1) * tq, p * LANES:(p + 1) * LANES].astype(F32)
            for hh in range(HEADS_PER_LANE_TILE):
                qm_ref[t, p * HEADS_PER_LANE_TILE + hh] = jnp.where(half == hh, qp, 0.0).astype(BF16)

    row = lax.broadcasted_iota(jnp.int32, (tq, tq), 0)
    col = lax.broadcasted_iota(jnp.int32, (tq, tq), 1)
    causal = col < row

    def emit(units):
        logit, suffix, a = {}, {}, {}

        def logits(u):
            t, hd, kb, _ = units[u]
            p = hd // HEADS_PER_LANE_TILE
            k0 = pl.multiple_of(kb * tq, tq)
            logit[u] = _dot_nt(qm_ref[t, hd], k_ref[pl.ds(k0, tq), p * LANES:(p + 1) * LANES])

        def suffix_sum(u):
            diagonal = units[u][3]
            z = logit[u].astype(BF16)
            sp = jnp.maximum(z, 0.0) + jnp.log(1.0 + jnp.exp(-jnp.abs(z)))
            if diagonal:
                sp = jnp.where(causal, sp, 0.0)
            suffix[u] = _dot(sp, tri_ref[...])

        def weights(u):
            t, hd, _, diagonal = units[u]
            log_a = logit.pop(u) + suffix[u]
            if not diagonal:
                log_a = log_a + carry_ref[t, hd]
            w = jnp.exp(log_a)
            if diagonal:
                w = jnp.where(causal, w, 0.0)
                carry_ref[t, hd] = suffix.pop(u)[:, 0:1]
            else:
                carry_ref[t, hd] = carry_ref[t, hd] + suffix.pop(u)[:, 0:1]
            a[u] = w.astype(BF16)

        def weighted_values(u_last):
            t, hd, kb, diagonal = units[u_last]
            grp = hd // PV_GROUP
            a_cat = jnp.concatenate([a.pop(u) for u in range(u_last - PV_GROUP + 1, u_last + 1)], axis=1)
            out = _dot(a_cat, vm_ref[kb, :, grp * PV_WIDTH:(grp + 1) * PV_WIDTH])
            if diagonal:
                acc_ref[t, grp] = out
            else:
                acc_ref[t, grp] = acc_ref[t, grp] + out

        n = len(units)
        for step in range(n + 2):
            if step < n:
                logits(step)
            if 0 <= step - 1 < n:
                suffix_sum(step - 1)
            if 0 <= step - 2 < n:
                weights(step - 2)
                if units[step - 2][1] % PV_GROUP == PV_GROUP - 1:
                    weighted_values(step - 2)

    def head_units(t, kb, diagonal):
        return [(t, hd, kb, diagonal) for hd in range(n_heads)]

    units = []
    for depth in range(n_tiles):
        for t in range(depth, n_tiles):
            units += head_units(t, first_block + t - depth, depth == 0)
    emit(units)

    def body(j, _):
        kb = first_block - 1 - j
        units = []
        for grp in range(n_heads // PV_GROUP):
            for t in range(n_tiles):
                units += [(t, grp * PV_GROUP + g, kb, False) for g in range(PV_GROUP)]
        emit(units)
        return 0

    lax.fori_loop(0, first_block, body, 0)

    for t in range(n_tiles):
        rs = slice(t * tq, (t + 1) * tq)
        for grp in range(n_heads // PV_GROUP):
            sl = slice(grp * PV_WIDTH, (grp + 1) * PV_WIDTH)
            mixed_ref[rs, sl] = (acc_ref[t, grp] * gate_ref[rs, sl].astype(F32)).astype(BF16)
    o_ref[...] = x_ref[...] + _dot(mixed_ref[...], wsb_ref[...]) + _dot(pm_ref[...], wpm_ref[...])


def _sb_out(q, k, v, gate, pm, x2d, tri, w_out_sb, w_out_pm, batch, seq_len):
    rows, sb_w = q.shape
    d = x2d.shape[1]
    n_heads = sb_w // HEAD_DIM
    step_rows = TQ * Q_TILES
    assert seq_len % step_rows == 0
    steps = seq_len // step_rows
    k3 = k.reshape(batch, seq_len, sb_w)
    v3 = v.reshape(batch, seq_len, sb_w)
    row_spec = lambda width: pl.BlockSpec((step_rows, width), lambda b, i: (b * steps + i, 0))
    seq_spec = pl.BlockSpec((None, seq_len, sb_w), lambda b, i: (b, 0, 0))
    full = lambda a: pl.BlockSpec(a.shape, lambda b, i: (0, 0))
    return pl.pallas_call(
        functools.partial(_sb_out_kernel, n_heads),
        out_shape=jax.ShapeDtypeStruct((rows, d), F32),
        grid=(batch, steps),
        in_specs=[row_spec(sb_w), seq_spec, seq_spec, row_spec(sb_w), row_spec(pm.shape[1]),
                  row_spec(d), full(tri), full(w_out_sb), full(w_out_pm)],
        out_specs=row_spec(d),
        scratch_shapes=[pltpu.VMEM((Q_TILES, n_heads, TQ, LANES), BF16),
                        pltpu.VMEM((seq_len // TQ, PV_GROUP * TQ, sb_w), BF16),
                        pltpu.VMEM((Q_TILES, n_heads // PV_GROUP, TQ, PV_WIDTH), F32),
                        pltpu.VMEM((Q_TILES, n_heads, TQ, 1), F32),
                        pltpu.VMEM((step_rows, sb_w), BF16)],
        compiler_params=pltpu.CompilerParams(dimension_semantics=("arbitrary", "arbitrary"),
                                             vmem_limit_bytes=VMEM_LIMIT),
        name="sb_out",
    )(q, k3, v3, gate, pm, x2d, tri, w_out_sb, w_out_pm)


def _block_diag(w):
    g, c, d = w.shape
    out = jnp.zeros((g * c, g * d), w.dtype)
    for i in range(g):
        out = out.at[i * c:(i + 1) * c, i * d:(i + 1) * d].set(w[i])
    return out


def kernel(x, mem, norm_g, w_in, pool_w, pool_scale, mem_norm_g, w_mem_kv, q_norm_g, k_norm_g, w_out):
    batch, seq_len, d = x.shape
    mem_len = mem.shape[1]
    depth = norm_g.shape[0]
    pool_width = pool_scale.shape[1]
    mem_width = w_mem_kv.shape[2] // 2
    assert seq_len % TQ == 0
    tri = jnp.where(lax.broadcasted_iota(jnp.int32, (TQ, TQ), 0) >= lax.broadcasted_iota(jnp.int32, (TQ, TQ), 1),
                    -1.0, 0.0).astype(BF16)
    x2d = x.reshape(batch * seq_len, d)
    for l in range(depth):
        mem_k, mem_v = _mem_kv(mem.reshape(batch * mem_len, d), mem_norm_g[l][None],
                               w_mem_kv[l].astype(BF16), jnp.tile(k_norm_g[l], MEM_HEADS)[None])
        q, k, v, gate, pm = _in_proj(x2d, norm_g[l][None], w_in[l].astype(BF16),
                                     _block_diag(pool_w[l]).astype(BF16), pool_scale[l][None],
                                     mem_k, mem_v, jnp.tile(q_norm_g[l], MEM_HEADS)[None],
                                     seq_len, mem_len)
        w_o = w_out[l].astype(BF16)
        w_out_sb = w_o[pool_width:w_o.shape[0] - mem_width]
        w_out_pm = jnp.concatenate([w_o[:pool_width], w_o[w_o.shape[0] - mem_width:]], axis=0)
        x2d = _sb_out(q, k, v, gate, pm, x2d, tri, w_out_sb, w_out_pm, batch, seq_len)
    return x2d.reshape(batch, seq_len, d)
```

```python
import functools

import jax
import jax.numpy as jnp
from jax import lax
from jax.experimental import pallas as pl
from jax.experimental.pallas import tpu as pltpu

F32 = jnp.float32
BF16 = jnp.bfloat16

EPS = 1e-6
HEAD_DIM = 64
POOL_WINDOWS = (2, 4, 8, 16)
MAX_WINDOW = max(POOL_WINDOWS)
MEM_HEADS = 4
LANES = 128
HEADS_PER_LANE_TILE = LANES // HEAD_DIM
PV_GROUP = HEADS_PER_LANE_TILE
PV_WIDTH = PV_GROUP * HEAD_DIM

TM = 1024
TM_SUB = 256
TQ = 256
Q_TILES = 4
VMEM_LIMIT = 56 * 1024 * 1024


def _silu(g):
    return g / (1.0 + jnp.exp(-g))


def _dot(a, b):
    return jnp.dot(a, b, preferred_element_type=F32)


def _dot_nt(a, b):
    return lax.dot_general(a, b, (((1,), (1,)), ((), ())), preferred_element_type=F32)


def _segment_mean_matrix(width, seg):
    r = lax.broadcasted_iota(jnp.int32, (width, width), 0) // seg
    c = lax.broadcasted_iota(jnp.int32, (width, width), 1) // seg
    return jnp.where(r == c, 1.0 / seg, 0.0).astype(BF16)


def _mem_kv_kernel(mem_ref, g_ref, w_ref, kg_ref, k_out, v_out):
    m = mem_ref[...]
    ms = jnp.mean(m * m, axis=-1, keepdims=True)
    h = (m * lax.rsqrt(ms + EPS) * g_ref[...]).astype(BF16)
    kv = _dot(h, w_ref[...])
    width = kv.shape[1] // 2
    k = kv[:, :width]
    v = kv[:, width:]
    seg = _segment_mean_matrix(width, width // MEM_HEADS)
    kms = _dot((k * k).astype(BF16), seg)
    k_out[...] = (k * lax.rsqrt(kms + EPS) * kg_ref[...]).astype(BF16)
    v_out[...] = v.astype(BF16)


def _mem_kv(mem2d, mem_norm_g, w_mem_kv, k_norm_g_tiled):
    rows, d = mem2d.shape
    width = w_mem_kv.shape[1] // 2
    tr = 512
    return pl.pallas_call(
        _mem_kv_kernel,
        out_shape=(jax.ShapeDtypeStruct((rows, width), BF16),
                   jax.ShapeDtypeStruct((rows, width), BF16)),
        grid=(rows // tr,),
        in_specs=[pl.BlockSpec((tr, d), lambda i: (i, 0)),
                  pl.BlockSpec((1, d), lambda i: (0, 0)),
                  pl.BlockSpec((d, 2 * width), lambda i: (0, 0)),
                  pl.BlockSpec((1, width), lambda i: (0, 0))],
        out_specs=(pl.BlockSpec((tr, width), lambda i: (i, 0)),
                   pl.BlockSpec((tr, width), lambda i: (i, 0))),
        compiler_params=pltpu.CompilerParams(dimension_semantics=("arbitrary",),
                                             vmem_limit_bytes=VMEM_LIMIT),
        name="mem_kv",
    )(mem2d, mem_norm_g, w_mem_kv, k_norm_g_tiled)


def _in_proj_kernel(seq_len, pool_w, sb_w, mem_w,
                    x_ref, g_ref, w_ref, pwbd_ref, pscale_ref, mk_ref, mv_ref, qg_ref,
                    q_out, k_out, v_out, gate_out, pm_out, pbuf, halo_ref):
    tm = x_ref.shape[0]
    n_sub = pbuf.shape[0]
    rows = tm // n_sub
    t0 = (pl.program_id(0) * tm) % seq_len
    head_w = mem_w // MEM_HEADS

    @pl.when(t0 == 0)
    def _():
        halo_ref[...] = jnp.zeros(halo_ref.shape, F32)

    lane = lax.broadcasted_iota(jnp.int32, (rows, pool_w), 1)
    group = lane // (pool_w // len(POOL_WINDOWS))
    row_iota = lax.broadcasted_iota(jnp.int32, (rows, pool_w), 0)
    seg = _segment_mean_matrix(mem_w, head_w)
    lane_q = lax.broadcasted_iota(jnp.int32, (rows, mem_w), 1) // head_w
    mv = mv_ref[...].astype(F32)
    lane_v = lax.broadcasted_iota(jnp.int32, mv.shape, 1) // head_w
    mv_heads = [jnp.where(lane_v == hd, mv, 0.0).astype(BF16) for hd in range(MEM_HEADS)]

    h, kept = {}, {}

    def normalize(s):
        x = x_ref[s * rows:(s + 1) * rows, :]
        ms = jnp.mean(x * x, axis=-1, keepdims=True)
        h[s] = (x * lax.rsqrt(ms + EPS) * g_ref[...]).astype(BF16)

    def project(s):
        hs = h.pop(s)
        rs = slice(s * rows, (s + 1) * rows)

        def proj(c0, width):
            return _dot(hs, w_ref[:, c0:c0 + width])

        c = 0
        pool_v = proj(c, pool_w); c += pool_w
        yield
        pool_g = proj(c, pool_w); c += pool_w
        yield
        q = proj(c, sb_w); c += sb_w
        q_out[rs, :] = (q * HEAD_DIM ** -0.5).astype(BF16)
        yield
        k_out[rs, :] = proj(c, sb_w).astype(BF16); c += sb_w
        yield
        v_out[rs, :] = proj(c, sb_w).astype(BF16); c += sb_w
        yield
        gate_out[rs, :] = _silu(proj(c, sb_w)).astype(BF16); c += sb_w
        yield
        mem_q = proj(c, mem_w); c += mem_w
        yield
        mem_g = proj(c, mem_w)
        kept[s] = (pool_v, pool_g, mem_q, mem_g)

    def mix(s):
        pool_v, pool_g, mem_q, mem_g = kept.pop(s)
        rs = slice(s * rows, (s + 1) * rows)

        pbuf[s, 0:MAX_WINDOW, :] = halo_ref[...]
        pbuf[s, MAX_WINDOW:MAX_WINDOW + rows, :] = pool_v
        halo_ref[...] = pool_v[rows - MAX_WINDOW:, :]
        t_seq = t0 + s * rows + row_iota
        win_sum = pool_v
        window = jnp.full((rows, pool_w), 1, jnp.int32)
        acc = pool_v
        shift = 1
        for g, w in enumerate(POOL_WINDOWS):
            while shift < w:
                acc = acc + pbuf[s, MAX_WINDOW - shift:MAX_WINDOW - shift + rows, :]
                shift += 1
            win_sum = jnp.where(group == g, acc, win_sum)
            window = jnp.where(group == g, w, window)
        count = jnp.minimum(t_seq + 1, window).astype(F32)
        pooled = (win_sum / count - pool_v).astype(BF16)
        y_pool = _dot(pooled, pwbd_ref[...])
        yield

        qms = _dot((mem_q * mem_q).astype(BF16), seg)
        yield
        pm_out[rs, 0:pool_w] = (y_pool * pscale_ref[...] * _silu(pool_g)).astype(BF16)
        qn = mem_q * lax.rsqrt(qms + EPS) * qg_ref[...]
        qn = qn * (head_w ** -0.5)

        def scores(hd):
            return _dot_nt(jnp.where(lane_q == hd, qn, 0.0).astype(BF16), mk_ref[...])

        sc_next = scores(0)
        yield
        y_mem = None
        for hd in range(MEM_HEADS):
            sc = sc_next
            if hd + 1 < MEM_HEADS:
                sc_next = scores(hd + 1)
                yield
            e = jnp.exp(sc - jnp.max(sc, axis=-1, keepdims=True))
            p = e * (1.0 / jnp.sum(e, axis=-1, keepdims=True))
            pv = _dot(p.astype(BF16), mv_heads[hd])
            yield
            y_mem = pv if y_mem is None else y_mem + pv
        pm_out[rs, pool_w:pool_w + mem_w] = (y_mem * _silu(mem_g)).astype(BF16)

    for step in range(n_sub + 2):
        if step < n_sub:
            normalize(step)
        stages = []
        if 0 <= step - 1 < n_sub:
            stages.append(project(step - 1))
        if 0 <= step - 2 < n_sub:
            stages.append(mix(step - 2))
        while stages:
            for stage in list(stages):
                if next(stage, stages) is stages:
                    stages.remove(stage)


def _in_proj(x2d, norm_g, w_in, pool_wbd, pool_scale, mem_k, mem_v, q_norm_g_tiled, seq_len, mem_len):
    rows, d = x2d.shape
    pool_w = pool_wbd.shape[0]
    mem_w = mem_k.shape[1]
    sb_w = (w_in.shape[1] - 2 * pool_w - 2 * mem_w) // 4
    assert seq_len % TM == 0 and rows % TM == 0
    tiles_per_seq = seq_len // TM
    row_spec = lambda width: pl.BlockSpec((TM, width), lambda i: (i, 0))
    full = lambda a: pl.BlockSpec(a.shape, lambda i: (0, 0))
    mem_spec = pl.BlockSpec((mem_len, mem_w), lambda i: (i // tiles_per_seq, 0))
    out_sd = lambda width: jax.ShapeDtypeStruct((rows, width), BF16)
    return pl.pallas_call(
        functools.partial(_in_proj_kernel, seq_len, pool_w, sb_w, mem_w),
        out_shape=(out_sd(sb_w), out_sd(sb_w), out_sd(sb_w), out_sd(sb_w), out_sd(pool_w + mem_w)),
        grid=(rows // TM,),
        in_specs=[row_spec(d), full(norm_g), full(w_in), full(pool_wbd), full(pool_scale),
                  mem_spec, mem_spec, full(q_norm_g_tiled)],
        out_specs=(row_spec(sb_w), row_spec(sb_w), row_spec(sb_w), row_spec(sb_w),
                   row_spec(pool_w + mem_w)),
        scratch_shapes=[pltpu.VMEM((TM // TM_SUB, MAX_WINDOW + TM_SUB, pool_w), F32),
                        pltpu.VMEM((MAX_WINDOW, pool_w), F32)],
        compiler_params=pltpu.CompilerParams(dimension_semantics=("arbitrary",),
                                             vmem_limit_bytes=VMEM_LIMIT),
        name="in_proj",
    )(x2d, norm_g, w_in, pool_wbd, pool_scale, mem_k, mem_v, q_norm_g_tiled)


def _sb_out_kernel(n_heads,
                   q_ref, k_ref, v_ref, gate_ref, pm_ref, x_ref, tri_ref, wsb_ref, wpm_ref,
                   o_ref, qm_ref, vm_ref, acc_ref, carry_ref, mixed_ref):
    tq = tri_ref.shape[0]
    n_tiles = q_ref.shape[0] // tq
    n_pairs = n_heads // HEADS_PER_LANE_TILE
    first_block = pl.program_id(1) * n_tiles
    half = lax.broadcasted_iota(jnp.int32, (1, LANES), 1) // HEAD_DIM

    @pl.when(pl.program_id(1) == 0)
    def _():
        lane_head = (lax.broadcasted_iota(jnp.int32, (tq, v_ref.shape[1]), 1) // HEAD_DIM) % PV_GROUP
        for kb in range(v_ref.shape[0] // tq):
            v = v_ref[kb * tq:(kb + 1) * tq, :].astype(F32)
            for g in range(PV_GROUP):
                vm_ref[kb, g * tq:(g + 1) * tq, :] = jnp.where(lane_head == g, v, 0.0).astype(BF16)

    for t in range(n_tiles):
        for p in range(n_pairs):
            qp = q_ref[t * tq:(t + 1) * tq, p * LANES:(p + 1) * LANES].astype(F32)
            for hh in range(HEADS_PER_LANE_TILE):
                qm_ref[t, p * HEADS_PER_LANE_TILE + hh] = jnp.where(half == hh, qp, 0.0).astype(BF16)

    row = lax.broadcasted_iota(jnp.int32, (tq, tq), 0)
    col = lax.broadcasted_iota(jnp.int32, (tq, tq), 1)
    causal = col < row

    def emit(units):
        logit, suffix, a = {}, {}, {}

        def logits(u):
            t, hd, kb, _ = units[u]
            p = hd // HEADS_PER_LANE_TILE
            k0 = pl.multiple_of(kb * tq, tq)
            logit[u] = _dot_nt(qm_ref[t, hd], k_ref[pl.ds(k0, tq), p * LANES:(p + 1) * LANES])

        def suffix_sum(u):
            diagonal = units[u][3]
            z = logit[u].astype(BF16)
            sp = jnp.maximum(z, 0.0) + jnp.log(1.0 + jnp.exp(-jnp.abs(z)))
            if diagonal:
                sp = jnp.where(causal, sp, 0.0)
            suffix[u] = _dot(sp, tri_ref[...])

        def weights(u):
            t, hd, _, diagonal = units[u]
            log_a = logit.pop(u) + suffix[u]
            if not diagonal:
                log_a = log_a + carry_ref[t, hd]
            w = jnp.exp(log_a)
            if diagonal:
                w = jnp.where(causal, w, 0.0)
                carry_ref[t, hd] = suffix.pop(u)[:, 0:1]
            else:
                carry_ref[t, hd] = carry_ref[t, hd] + suffix.pop(u)[:, 0:1]
            a[u] = w.astype(BF16)

        def weighted_values(u_last):
            t, hd, kb, diagonal = units[u_last]
            grp = hd // PV_GROUP
            a_cat = jnp.concatenate([a.pop(u) for u in range(u_last - PV_GROUP + 1, u_last + 1)], axis=1)
            out = _dot(a_cat, vm_ref[kb, :, grp * PV_WIDTH:(grp + 1) * PV_WIDTH])
            if diagonal:
                acc_ref[t, grp] = out
            else:
                acc_ref[t, grp] = acc_ref[t, grp] + out

        n = len(units)
        for step in range(n + 2):
            if step < n:
                logits(step)
            if 0 <= step - 1 < n:
                suffix_sum(step - 1)
            if 0 <= step - 2 < n:
                weights(step - 2)
                if units[step - 2][1] % PV_GROUP == PV_GROUP - 1:
                    weighted_values(step - 2)

    def head_units(t, kb, diagonal):
        return [(t, hd, kb, diagonal) for hd in range(n_heads)]

    units = []
    for depth in range(n_tiles):
        for t in range(depth, n_tiles):
            units += head_units(t, first_block + t - depth, depth == 0)
    emit(units)

    def body(j, _):
        kb = first_block - 1 - j
        units = []
        for grp in range(n_heads // PV_GROUP):
            for t in range(n_tiles):
                units += [(t, grp * PV_GROUP + g, kb, False) for g in range(PV_GROUP)]
        emit(units)
        return 0

    lax.fori_loop(0, first_block, body, 0)

    for t in range(n_tiles):
        rs = slice(t * tq, (t + 1) * tq)
        for grp in range(n_heads // PV_GROUP):
            sl = slice(grp * PV_WIDTH, (grp + 1) * PV_WIDTH)
            mixed_ref[rs, sl] = (acc_ref[t, grp] * gate_ref[rs, sl].astype(F32)).astype(BF16)
    o_ref[...] = x_ref[...] + _dot(mixed_ref[...], wsb_ref[...]) + _dot(pm_ref[...], wpm_ref[...])


def _sb_out(q, k, v, gate, pm, x2d, tri, w_out_sb, w_out_pm, batch, seq_len):
    rows, sb_w = q.shape
    d = x2d.shape[1]
    n_heads = sb_w // HEAD_DIM
    step_rows = TQ * Q_TILES
    assert seq_len % step_rows == 0
    steps = seq_len // step_rows
    k3 = k.reshape(batch, seq_len, sb_w)
    v3 = v.reshape(batch, seq_len, sb_w)
    row_spec = lambda width: pl.BlockSpec((step_rows, width), lambda b, i: (b * steps + i, 0))
    seq_spec = pl.BlockSpec((None, seq_len, sb_w), lambda b, i: (b, 0, 0))
    full = lambda a: pl.BlockSpec(a.shape, lambda b, i: (0, 0))
    return pl.pallas_call(
        functools.partial(_sb_out_kernel, n_heads),
        out_shape=jax.ShapeDtypeStruct((rows, d), F32),
        grid=(batch, steps),
        in_specs=[row_spec(sb_w), seq_spec, seq_spec, row_spec(sb_w), row_spec(pm.shape[1]),
                  row_spec(d), full(tri), full(w_out_sb), full(w_out_pm)],
        out_specs=row_spec(d),
        scratch_shapes=[pltpu.VMEM((Q_TILES, n_heads, TQ, LANES), BF16),
                        pltpu.VMEM((seq_len // TQ, PV_GROUP * TQ, sb_w), BF16),
                        pltpu.VMEM((Q_TILES, n_heads // PV_GROUP, TQ, PV_WIDTH), F32),
                        pltpu.VMEM((Q_TILES, n_heads, TQ, 1), F32),
                        pltpu.VMEM((step_rows, sb_w), BF16)],
        compiler_params=pltpu.CompilerParams(dimension_semantics=("arbitrary", "arbitrary"),
                                             vmem_limit_bytes=VMEM_LIMIT),
        name="sb_out",
    )(q, k3, v3, gate, pm, x2d, tri, w_out_sb, w_out_pm)


def _block_diag(w):
    g, c, d = w.shape
    out = jnp.zeros((g * c, g * d), w.dtype)
    for i in range(g):
        out = out.at[i * c:(i + 1) * c, i * d:(i + 1) * d].set(w[i])
    return out


def kernel(x, mem, norm_g, w_in, pool_w, pool_scale, mem_norm_g, w_mem_kv, q_norm_g, k_norm_g, w_out):
    batch, seq_len, d = x.shape
    mem_len = mem.shape[1]
    depth = norm_g.shape[0]
    pool_width = pool_scale.shape[1]
    mem_width = w_mem_kv.shape[2] // 2
    assert seq_len % TQ == 0
    tri = jnp.where(lax.broadcasted_iota(jnp.int32, (TQ, TQ), 0) >= lax.broadcasted_iota(jnp.int32, (TQ, TQ), 1),
                    -1.0, 0.0).astype(BF16)
    x2d = x.reshape(batch * seq_len, d)
    for l in range(depth):
        mem_k, mem_v = _mem_kv(mem.reshape(batch * mem_len, d), mem_norm_g[l][None],
                               w_mem_kv[l].astype(BF16), jnp.tile(k_norm_g[l], MEM_HEADS)[None])
        q, k, v, gate, pm = _in_proj(x2d, norm_g[l][None], w_in[l].astype(BF16),
                                     _block_diag(pool_w[l]).astype(BF16), pool_scale[l][None],
                                     mem_k, mem_v, jnp.tile(q_norm_g[l], MEM_HEADS)[None],
                                     seq_len, mem_len)
        w_o = w_out[l].astype(BF16)
        w_out_sb = w_o[pool_width:w_o.shape[0] - mem_width]
        w_out_pm = jnp.concatenate([w_o[:pool_width], w_o[w_o.shape[0] - mem_width:]], axis=0)
        x2d = _sb_out(q, k, v, gate, pm, x2d, tri, w_out_sb, w_out_pm, batch, seq_len)
    return x2d.reshape(batch, seq_len, d)
```

```python
import functools

import jax
import jax.numpy as jnp
from jax import lax
from jax.experimental import pallas as pl
from jax.experimental.pallas import tpu as pltpu

F32 = jnp.float32
BF16 = jnp.bfloat16

EPS = 1e-6
HEAD_DIM = 64
POOL_WINDOWS = (2, 4, 8, 16)
MAX_WINDOW = max(POOL_WINDOWS)
MEM_HEADS = 4
LANES = 128
HEADS_PER_LANE_TILE = LANES // HEAD_DIM
PV_GROUP = HEADS_PER_LANE_TILE
PV_WIDTH = PV_GROUP * HEAD_DIM

TM = 1024
TM_SUB = 256
TQ = 256
Q_TILES = 4
VMEM_LIMIT = 56 * 1024 * 1024


def _silu(g):
    return g / (1.0 + jnp.exp(-g))


def _dot(a, b):
    return jnp.dot(a, b, preferred_element_type=F32)


def _dot_nt(a, b):
    return lax.dot_general(a, b, (((1,), (1,)), ((), ())), preferred_element_type=F32)


def _per_head(g):
    return jnp.concatenate([g] * MEM_HEADS, axis=1)


def _block_diag(w_ref):
    g, c, _ = w_ref.shape
    zero = jnp.zeros((c, c), F32)
    rows = [jnp.concatenate([w_ref[i] if j == i else zero for j in range(g)], axis=1) for i in range(g)]
    return jnp.concatenate(rows, axis=0).astype(BF16)


def _segment_mean_matrix(width, seg):
    r = lax.broadcasted_iota(jnp.int32, (width, width), 0) // seg
    c = lax.broadcasted_iota(jnp.int32, (width, width), 1) // seg
    return jnp.where(r == c, 1.0 / seg, 0.0).astype(BF16)


def _mem_kv_kernel(mem_ref, g_ref, w_ref, kg_ref, k_out, v_out):
    m = mem_ref[...]
    ms = jnp.mean(m * m, axis=-1, keepdims=True)
    h = (m * lax.rsqrt(ms + EPS) * g_ref[...]).astype(BF16)
    kv = _dot(h, w_ref[...])
    width = kv.shape[1] // 2
    k = kv[:, :width]
    v = kv[:, width:]
    seg = _segment_mean_matrix(width, width // MEM_HEADS)
    kms = _dot((k * k).astype(BF16), seg)
    k_out[...] = (k * lax.rsqrt(kms + EPS) * _per_head(kg_ref[...])).astype(BF16)
    v_out[...] = v.astype(BF16)


def _mem_kv(mem2d, mem_norm_g, w_mem_kv, k_norm_g):
    rows, d = mem2d.shape
    width = w_mem_kv.shape[1] // 2
    tr = 512
    return pl.pallas_call(
        _mem_kv_kernel,
        out_shape=(jax.ShapeDtypeStruct((rows, width), BF16),
                   jax.ShapeDtypeStruct((rows, width), BF16)),
        grid=(rows // tr,),
        in_specs=[pl.BlockSpec((tr, d), lambda i: (i, 0)),
                  pl.BlockSpec((1, d), lambda i: (0, 0)),
                  pl.BlockSpec((d, 2 * width), lambda i: (0, 0)),
                  pl.BlockSpec(k_norm_g.shape, lambda i: (0, 0))],
        out_specs=(pl.BlockSpec((tr, width), lambda i: (i, 0)),
                   pl.BlockSpec((tr, width), lambda i: (i, 0))),
        compiler_params=pltpu.CompilerParams(dimension_semantics=("arbitrary",),
                                             vmem_limit_bytes=VMEM_LIMIT),
        name="mem_kv",
    )(mem2d, mem_norm_g, w_mem_kv, k_norm_g)


DONE, MIX_READY = object(), object()

def _in_proj_kernel(seq_len, pool_w, sb_w, mem_w,
                    x_ref, g_ref, w_ref, pw_ref, pscale_ref, mk_ref, mv_ref, qg_ref,
                    q_out, k_out, v_out, gate_out, pm_out, pbuf, halo_ref):
    tm = x_ref.shape[0]
    n_sub = pbuf.shape[0]
    rows = tm // n_sub
    t0 = (pl.program_id(0) * tm) % seq_len
    head_w = mem_w // MEM_HEADS

    @pl.when(t0 == 0)
    def _():
        halo_ref[...] = jnp.zeros(halo_ref.shape, F32)

    lane = lax.broadcasted_iota(jnp.int32, (rows, pool_w), 1)
    group = lane // (pool_w // len(POOL_WINDOWS))
    row_iota = lax.broadcasted_iota(jnp.int32, (rows, pool_w), 0)
    seg = _segment_mean_matrix(mem_w, head_w)
    q_gain = _per_head(qg_ref[...])
    pool_mix = _block_diag(pw_ref)
    lane_q = lax.broadcasted_iota(jnp.int32, (rows, mem_w), 1) // head_w
    mv = mv_ref[...].astype(F32)
    lane_v = lax.broadcasted_iota(jnp.int32, mv.shape, 1) // head_w
    mv_heads = [jnp.where(lane_v == hd, mv, 0.0).astype(BF16) for hd in range(MEM_HEADS)]

    h, kept = {}, {}

    def normalize(s, n_chunks=4):
        chunk = rows // n_chunks
        parts = []
        for i in range(n_chunks):
            x = x_ref[s * rows + i * chunk:s * rows + (i + 1) * chunk, :]
            ms = jnp.mean(x * x, axis=-1, keepdims=True)
            parts.append((x * lax.rsqrt(ms + EPS) * g_ref[...]).astype(BF16))
            if i + 1 < n_chunks:
                yield
        h[s] = jnp.concatenate(parts, axis=0)

    def project(s):
        hs = h.pop(s)
        rs = slice(s * rows, (s + 1) * rows)

        def proj(c0, width):
            return _dot(hs, w_ref[:, c0:c0 + width])

        c_pool, c_sb, c_mem = 0, 2 * pool_w, 2 * pool_w + 4 * sb_w
        pool_v = proj(c_pool, pool_w)
        yield
        pool_g = proj(c_pool + pool_w, pool_w)
        yield
        mem_q = proj(c_mem, mem_w)
        yield
        mem_g = proj(c_mem + mem_w, mem_w)
        kept[s] = (pool_v, pool_g, mem_q, mem_g)
        yield MIX_READY
        q_out[rs, :] = (proj(c_sb, sb_w) * HEAD_DIM ** -0.5).astype(BF16)
        yield
        k_out[rs, :] = proj(c_sb + sb_w, sb_w).astype(BF16)
        yield
        v_out[rs, :] = proj(c_sb + 2 * sb_w, sb_w).astype(BF16)
        yield
        gate_out[rs, :] = _silu(proj(c_sb + 3 * sb_w, sb_w)).astype(BF16)

    def mix(s):
        pool_v, pool_g, mem_q, mem_g = kept.pop(s)
        rs = slice(s * rows, (s + 1) * rows)

        pbuf[s, 0:MAX_WINDOW, :] = halo_ref[...]
        pbuf[s, MAX_WINDOW:MAX_WINDOW + rows, :] = pool_v
        halo_ref[...] = pool_v[rows - MAX_WINDOW:, :]
        t_seq = t0 + s * rows + row_iota
        win_sum = pool_v
        window = jnp.full((rows, pool_w), 1, jnp.int32)
        acc = pool_v
        shift = 1
        for g, w in enumerate(POOL_WINDOWS):
            while shift < w:
                acc = acc + pbuf[s, MAX_WINDOW - shift:MAX_WINDOW - shift + rows, :]
                shift += 1
            win_sum = jnp.where(group == g, acc, win_sum)
            window = jnp.where(group == g, w, window)
        count = jnp.minimum(t_seq + 1, window).astype(F32)
        pooled = (win_sum / count - pool_v).astype(BF16)
        y_pool = _dot(pooled, pool_mix)
        yield

        qms = _dot((mem_q * mem_q).astype(BF16), seg)
        yield
        pm_out[rs, 0:pool_w] = (y_pool * pscale_ref[...] * _silu(pool_g)).astype(BF16)
        qn = mem_q * lax.rsqrt(qms + EPS) * q_gain
        qn = qn * (head_w ** -0.5)

        scores = []
        for hd in range(MEM_HEADS):
            scores.append(_dot_nt(jnp.where(lane_q == hd, qn, 0.0).astype(BF16), mk_ref[...]))
            yield
        y_mem = None
        for hd in range(MEM_HEADS):
            e = jnp.exp(scores[hd] - jnp.max(scores[hd], axis=-1, keepdims=True))
            pv = _dot(e.astype(BF16), mv_heads[hd])
            yield
            pv = pv * (1.0 / jnp.sum(e, axis=-1, keepdims=True))
            y_mem = pv if y_mem is None else y_mem + pv
        pm_out[rs, pool_w:pool_w + mem_w] = (y_mem * _silu(mem_g)).astype(BF16)

    def sub_tile(s):
        yield from normalize(s)
        yield from project(s)

    stages = [(0, sub_tile(0))]
    while stages:
        for entry in list(stages):
            s, stage = entry
            event = next(stage, DONE)
            if event is DONE:
                stages.remove(entry)
            elif event is MIX_READY:
                stages.append((s, mix(s)))
                if s + 1 < n_sub:
                    stages.append((s + 1, sub_tile(s + 1)))


def _in_proj(x2d, norm_g, w_in, pool_w3, pool_scale, mem_k, mem_v, q_norm_g, seq_len, mem_len):
    rows, d = x2d.shape
    pool_w = pool_w3.shape[0] * pool_w3.shape[1]
    mem_w = mem_k.shape[1]
    sb_w = (w_in.shape[1] - 2 * pool_w - 2 * mem_w) // 4
    assert seq_len % TM == 0 and rows % TM == 0
    tiles_per_seq = seq_len // TM
    row_spec = lambda width: pl.BlockSpec((TM, width), lambda i: (i, 0))
    full = lambda a: pl.BlockSpec(a.shape, lambda i: (0,) * a.ndim)
    mem_spec = pl.BlockSpec((mem_len, mem_w), lambda i: (i // tiles_per_seq, 0))
    out_sd = lambda width: jax.ShapeDtypeStruct((rows, width), BF16)
    return pl.pallas_call(
        functools.partial(_in_proj_kernel, seq_len, pool_w, sb_w, mem_w),
        out_shape=(out_sd(sb_w), out_sd(sb_w), out_sd(sb_w), out_sd(sb_w), out_sd(pool_w + mem_w)),
        grid=(rows // TM,),
        in_specs=[row_spec(d), full(norm_g), full(w_in), full(pool_w3), full(pool_scale),
                  mem_spec, mem_spec, full(q_norm_g)],
        out_specs=(row_spec(sb_w), row_spec(sb_w), row_spec(sb_w), row_spec(sb_w),
                   row_spec(pool_w + mem_w)),
        scratch_shapes=[pltpu.VMEM((TM // TM_SUB, MAX_WINDOW + TM_SUB, pool_w), F32),
                        pltpu.VMEM((MAX_WINDOW, pool_w), F32)],
        compiler_params=pltpu.CompilerParams(dimension_semantics=("arbitrary",),
                                             vmem_limit_bytes=VMEM_LIMIT),
        name="in_proj",
    )(x2d, norm_g, w_in, pool_w3, pool_scale, mem_k, mem_v, q_norm_g)


def _sb_out_kernel(n_heads, pool_w,
                   q_ref, k_ref, v_ref, gate_ref, pm_ref, x_ref, w_ref,
                   o_ref, tri_ref, qm_ref, vm_ref, acc_ref, carry_ref, mixed_ref):
    tq = tri_ref.shape[0]
    n_tiles = q_ref.shape[0] // tq
    n_pairs = n_heads // HEADS_PER_LANE_TILE
    first_block = pl.program_id(1) * n_tiles
    half = lax.broadcasted_iota(jnp.int32, (1, LANES), 1) // HEAD_DIM

    @pl.when(pl.program_id(1) == 0)
    def _():
        lane_head = (lax.broadcasted_iota(jnp.int32, (tq, v_ref.shape[1]), 1) // HEAD_DIM) % PV_GROUP
        for kb in range(v_ref.shape[0] // tq):
            v = v_ref[kb * tq:(kb + 1) * tq, :].astype(F32)
            for g in range(PV_GROUP):
                vm_ref[kb, g * tq:(g + 1) * tq, :] = jnp.where(lane_head == g, v, 0.0).astype(BF16)

    for t in range(n_tiles):
        for p in range(n_pairs):
            qp = q_ref[t * tq:(t + 1) * tq, p * LANES:(p + 1) * LANES].astype(F32)
            for hh in range(HEADS_PER_LANE_TILE):
                qm_ref[t, p * HEADS_PER_LANE_TILE + hh] = jnp.where(half == hh, qp, 0.0).astype(BF16)

    row = lax.broadcasted_iota(jnp.int32, (tq, tq), 0)
    col = lax.broadcasted_iota(jnp.int32, (tq, tq), 1)
    causal = col < row
    tri_ref[...] = jnp.where(row >= col, -1.0, 0.0).astype(BF16)

    def emit(units):
        logit, suffix, a = {}, {}, {}

        def logits(u):
            t, hd, kb, _ = units[u]
            p = hd // HEADS_PER_LANE_TILE
            k0 = pl.multiple_of(kb * tq, tq)
            logit[u] = _dot_nt(qm_ref[t, hd], k_ref[pl.ds(k0, tq), p * LANES:(p + 1) * LANES])

        def suffix_sum(u):
            diagonal = units[u][3]
            z = logit[u].astype(BF16)
            sp = jnp.maximum(z, 0.0) + jnp.log(1.0 + jnp.exp(-jnp.abs(z)))
            if diagonal:
                sp = jnp.where(causal, sp, 0.0)
            suffix[u] = _dot(sp, tri_ref[...])

        def weights(u):
            t, hd, _, diagonal = units[u]
            log_a = logit.pop(u) + suffix[u]
            if not diagonal:
                log_a = log_a + carry_ref[t, hd]
            w = jnp.exp(log_a)
            if diagonal:
                w = jnp.where(causal, w, 0.0)
                carry_ref[t, hd] = suffix.pop(u)[:, 0:1]
            else:
                carry_ref[t, hd] = carry_ref[t, hd] + suffix.pop(u)[:, 0:1]
            a[u] = w.astype(BF16)

        def weighted_values(u_last):
            t, hd, kb, diagonal = units[u_last]
            grp = hd // PV_GROUP
            a_cat = jnp.concatenate([a.pop(u) for u in range(u_last - PV_GROUP + 1, u_last + 1)], axis=1)
            out = _dot(a_cat, vm_ref[kb, :, grp * PV_WIDTH:(grp + 1) * PV_WIDTH])
            if diagonal:
                acc_ref[t, grp] = out
            else:
                acc_ref[t, grp] = acc_ref[t, grp] + out

        n = len(units)
        for step in range(n + 2):
            if step < n:
                logits(step)
            if 0 <= step - 1 < n:
                suffix_sum(step - 1)
            if 0 <= step - 2 < n:
                weights(step - 2)
                if units[step - 2][1] % PV_GROUP == PV_GROUP - 1:
                    weighted_values(step - 2)

    def head_units(t, kb, diagonal):
        return [(t, hd, kb, diagonal) for hd in range(n_heads)]

    units = []
    for depth in range(n_tiles):
        for t in range(depth, n_tiles):
            units += head_units(t, first_block + t - depth, depth == 0)
    emit(units)

    def body(j, _):
        kb = first_block - 1 - j
        units = []
        for grp in range(n_heads // PV_GROUP):
            for t in range(n_tiles):
                units += [(t, grp * PV_GROUP + g, kb, False) for g in range(PV_GROUP)]
        emit(units)
        return 0

    lax.fori_loop(0, first_block, body, 0)

    for t in range(n_tiles):
        rs = slice(t * tq, (t + 1) * tq)
        for grp in range(n_heads // PV_GROUP):
            sl = slice(grp * PV_WIDTH, (grp + 1) * PV_WIDTH)
            mixed_ref[rs, sl] = (acc_ref[t, grp] * gate_ref[rs, sl].astype(F32)).astype(BF16)
    sb_w = mixed_ref.shape[1]
    o_ref[...] = (x_ref[...] + _dot(mixed_ref[...], w_ref[pool_w:pool_w + sb_w, :])
                  + _dot(pm_ref[:, :pool_w], w_ref[:pool_w, :])
                  + _dot(pm_ref[:, pool_w:], w_ref[pool_w + sb_w:, :]))


def _sb_out(q, k, v, gate, pm, x2d, w_out, pool_w, batch, seq_len):
    rows, sb_w = q.shape
    d = x2d.shape[1]
    n_heads = sb_w // HEAD_DIM
    step_rows = TQ * Q_TILES
    assert seq_len % step_rows == 0
    steps = seq_len // step_rows
    k3 = k.reshape(batch, seq_len, sb_w)
    v3 = v.reshape(batch, seq_len, sb_w)
    row_spec = lambda width: pl.BlockSpec((step_rows, width), lambda b, i: (b * steps + i, 0))
    seq_spec = pl.BlockSpec((None, seq_len, sb_w), lambda b, i: (b, 0, 0))
    full = lambda a: pl.BlockSpec(a.shape, lambda b, i: (0, 0))
    return pl.pallas_call(
        functools.partial(_sb_out_kernel, n_heads, pool_w),
        out_shape=jax.ShapeDtypeStruct((rows, d), F32),
        grid=(batch, steps),
        in_specs=[row_spec(sb_w), seq_spec, seq_spec, row_spec(sb_w), row_spec(pm.shape[1]),
                  row_spec(d), full(w_out)],
        out_specs=row_spec(d),
        scratch_shapes=[pltpu.VMEM((TQ, TQ), BF16),
                        pltpu.VMEM((Q_TILES, n_heads, TQ, LANES), BF16),
                        pltpu.VMEM((seq_len // TQ, PV_GROUP * TQ, sb_w), BF16),
                        pltpu.VMEM((Q_TILES, n_heads // PV_GROUP, TQ, PV_WIDTH), F32),
                        pltpu.VMEM((Q_TILES, n_heads, TQ, 1), F32),
                        pltpu.VMEM((step_rows, sb_w), BF16)],
        compiler_params=pltpu.CompilerParams(dimension_semantics=("arbitrary", "arbitrary"),
                                             vmem_limit_bytes=VMEM_LIMIT),
        name="sb_out",
    )(q, k3, v3, gate, pm, x2d, w_out)


def kernel(x, mem, norm_g, w_in, pool_w, pool_scale, mem_norm_g, w_mem_kv, q_norm_g, k_norm_g, w_out):
    batch, seq_len, d = x.shape
    mem_len = mem.shape[1]
    depth = norm_g.shape[0]
    pool_width = pool_scale.shape[1]
    assert seq_len % TQ == 0
    x2d = x.reshape(batch * seq_len, d)
    for l in range(depth):
        mem_k, mem_v = _mem_kv(mem.reshape(batch * mem_len, d), mem_norm_g[l][None],
                               w_mem_kv[l].astype(BF16), k_norm_g[l][None])
        q, k, v, gate, pm = _in_proj(x2d, norm_g[l][None], w_in[l].astype(BF16), pool_w[l],
                                     pool_scale[l][None], mem_k, mem_v, q_norm_g[l][None],
                                     seq_len, mem_len)
        x2d = _sb_out(q, k, v, gate, pm, x2d, w_out[l].astype(BF16), pool_width, batch, seq_len)
    return x2d.reshape(batch, seq_len, d)
```

```python
import functools

import jax
import jax.numpy as jnp
from jax import lax
from jax.experimental import pallas as pl
from jax.experimental.pallas import tpu as pltpu

F32 = jnp.float32
BF16 = jnp.bfloat16

EPS = 1e-6
HEAD_DIM = 64
POOL_WINDOWS = (2, 4, 8, 16)
MAX_WINDOW = max(POOL_WINDOWS)
MEM_HEADS = 4
LANES = 128
HEADS_PER_LANE_TILE = LANES // HEAD_DIM
PV_GROUP = HEADS_PER_LANE_TILE
PV_WIDTH = PV_GROUP * HEAD_DIM

TM = 512
TM_SUB = 256
TQ = 256
Q_TILES = 4
VMEM_LIMIT = 56 * 1024 * 1024


def _silu(g):
    return g / (1.0 + jnp.exp(-g))


def _dot(a, b):
    return jnp.dot(a, b, preferred_element_type=F32)


def _dot_nt(a, b):
    return lax.dot_general(a, b, (((1,), (1,)), ((), ())), preferred_element_type=F32)


def _per_head(g):
    return jnp.concatenate([g] * MEM_HEADS, axis=1)


def _block_diag(w_ref):
    g, c, _ = w_ref.shape
    zero = jnp.zeros((c, c), F32)
    rows = [jnp.concatenate([w_ref[i] if j == i else zero for j in range(g)], axis=1) for i in range(g)]
    return jnp.concatenate(rows, axis=0).astype(BF16)


def _segment_mean_matrix(width, seg):
    r = lax.broadcasted_iota(jnp.int32, (width, width), 0) // seg
    c = lax.broadcasted_iota(jnp.int32, (width, width), 1) // seg
    return jnp.where(r == c, 1.0 / seg, 0.0).astype(BF16)


def _mem_kv_kernel(mem_ref, g_ref, w_ref, kg_ref, k_out, v_out):
    m = mem_ref[...]
    ms = jnp.mean(m * m, axis=-1, keepdims=True)
    h = (m * lax.rsqrt(ms + EPS) * g_ref[...]).astype(BF16)
    kv = _dot(h, w_ref[...])
    width = kv.shape[1] // 2
    k = kv[:, :width]
    v = kv[:, width:]
    seg = _segment_mean_matrix(width, width // MEM_HEADS)
    kms = _dot((k * k).astype(BF16), seg)
    k_out[...] = (k * lax.rsqrt(kms + EPS) * _per_head(kg_ref[...])).astype(BF16)
    v_out[...] = v.astype(BF16)


def _mem_kv(mem2d, mem_norm_g, w_mem_kv, k_norm_g):
    rows, d = mem2d.shape
    width = w_mem_kv.shape[1] // 2
    tr = 512
    return pl.pallas_call(
        _mem_kv_kernel,
        out_shape=(jax.ShapeDtypeStruct((rows, width), BF16),
                   jax.ShapeDtypeStruct((rows, width), BF16)),
        grid=(rows // tr,),
        in_specs=[pl.BlockSpec((tr, d), lambda i: (i, 0)),
                  pl.BlockSpec((1, d), lambda i: (0, 0)),
                  pl.BlockSpec((d, 2 * width), lambda i: (0, 0)),
                  pl.BlockSpec(k_norm_g.shape, lambda i: (0, 0))],
        out_specs=(pl.BlockSpec((tr, width), lambda i: (i, 0)),
                   pl.BlockSpec((tr, width), lambda i: (i, 0))),
        compiler_params=pltpu.CompilerParams(dimension_semantics=("arbitrary",),
                                             vmem_limit_bytes=VMEM_LIMIT),
        name="mem_kv",
    )(mem2d, mem_norm_g, w_mem_kv, k_norm_g)


DONE, MIX_READY = object(), object()

def _in_proj_kernel(seq_len, pool_w, sb_w, mem_w,
                    x_ref, g_ref, w_ref, pw_ref, pscale_ref, mk_ref, mv_ref, qg_ref,
                    q_out, k_out, v_out, gate_out, pm_out, pbuf, halo_ref):
    tm = x_ref.shape[0]
    n_sub = pbuf.shape[0]
    rows = tm // n_sub
    t0 = (pl.program_id(0) * tm) % seq_len
    head_w = mem_w // MEM_HEADS

    @pl.when(t0 == 0)
    def _():
        halo_ref[...] = jnp.zeros(halo_ref.shape, F32)

    lane = lax.broadcasted_iota(jnp.int32, (rows, pool_w), 1)
    group = lane // (pool_w // len(POOL_WINDOWS))
    row_iota = lax.broadcasted_iota(jnp.int32, (rows, pool_w), 0)
    seg = _segment_mean_matrix(mem_w, head_w)
    q_gain = _per_head(qg_ref[...])
    pool_mix = _block_diag(pw_ref)
    lane_q = lax.broadcasted_iota(jnp.int32, (rows, mem_w), 1) // head_w
    mv = mv_ref[...].astype(F32)
    lane_v = lax.broadcasted_iota(jnp.int32, mv.shape, 1) // head_w
    mv_heads = [jnp.where(lane_v == hd, mv, 0.0).astype(BF16) for hd in range(MEM_HEADS)]

    h, kept = {}, {}

    def normalize(s, n_chunks=4):
        chunk = rows // n_chunks
        parts = []
        for i in range(n_chunks):
            x = x_ref[s * rows + i * chunk:s * rows + (i + 1) * chunk, :]
            ms = jnp.mean(x * x, axis=-1, keepdims=True)
            parts.append((x * lax.rsqrt(ms + EPS) * g_ref[...]).astype(BF16))
            if i + 1 < n_chunks:
                yield
        h[s] = jnp.concatenate(parts, axis=0)

    def project(s):
        hs = h.pop(s)
        rs = slice(s * rows, (s + 1) * rows)

        def proj(c0, width):
            return _dot(hs, w_ref[:, c0:c0 + width])

        c_pool, c_sb, c_mem = 0, 2 * pool_w, 2 * pool_w + 4 * sb_w
        pool_v = proj(c_pool, pool_w)
        yield
        pool_g = proj(c_pool + pool_w, pool_w)
        yield
        mem_q = proj(c_mem, mem_w)
        yield
        mem_g = proj(c_mem + mem_w, mem_w)
        kept[s] = (pool_v, pool_g, mem_q, mem_g)
        yield MIX_READY
        q_out[rs, :] = (proj(c_sb, sb_w) * HEAD_DIM ** -0.5).astype(BF16)
        yield
        k_out[rs, :] = proj(c_sb + sb_w, sb_w).astype(BF16)
        yield
        v_out[rs, :] = proj(c_sb + 2 * sb_w, sb_w).astype(BF16)
        yield
        gate_out[rs, :] = _silu(proj(c_sb + 3 * sb_w, sb_w)).astype(BF16)

    def mix(s):
        pool_v, pool_g, mem_q, mem_g = kept.pop(s)
        rs = slice(s * rows, (s + 1) * rows)

        pbuf[s, 0:MAX_WINDOW, :] = halo_ref[...]
        pbuf[s, MAX_WINDOW:MAX_WINDOW + rows, :] = pool_v
        halo_ref[...] = pool_v[rows - MAX_WINDOW:, :]
        t_seq = t0 + s * rows + row_iota
        win_sum = pool_v
        window = jnp.full((rows, pool_w), 1, jnp.int32)
        acc = pool_v
        shift = 1
        for g, w in enumerate(POOL_WINDOWS):
            while shift < w:
                acc = acc + pbuf[s, MAX_WINDOW - shift:MAX_WINDOW - shift + rows, :]
                shift += 1
            win_sum = jnp.where(group == g, acc, win_sum)
            window = jnp.where(group == g, w, window)
        count = jnp.minimum(t_seq + 1, window).astype(F32)
        pooled = (win_sum / count - pool_v).astype(BF16)
        y_pool = _dot(pooled, pool_mix)
        yield

        qms = _dot((mem_q * mem_q).astype(BF16), seg)
        yield
        pm_out[rs, 0:pool_w] = (y_pool * pscale_ref[...] * _silu(pool_g)).astype(BF16)
        qn = mem_q * lax.rsqrt(qms + EPS) * q_gain
        qn = qn * (head_w ** -0.5)

        scores = []
        for hd in range(MEM_HEADS):
            scores.append(_dot_nt(jnp.where(lane_q == hd, qn, 0.0).astype(BF16), mk_ref[...]))
            yield
        y_mem = None
        for hd in range(MEM_HEADS):
            e = jnp.exp(scores[hd] - jnp.max(scores[hd], axis=-1, keepdims=True))
            pv = _dot(e.astype(BF16), mv_heads[hd])
            yield
            pv = pv * (1.0 / jnp.sum(e, axis=-1, keepdims=True))
            y_mem = pv if y_mem is None else y_mem + pv
        pm_out[rs, pool_w:pool_w + mem_w] = (y_mem * _silu(mem_g)).astype(BF16)

    def sub_tile(s):
        yield from normalize(s)
        yield from project(s)

    stages = [(0, sub_tile(0))]
    while stages:
        for entry in list(stages):
            s, stage = entry
            event = next(stage, DONE)
            if event is DONE:
                stages.remove(entry)
            elif event is MIX_READY:
                stages.append((s, mix(s)))
                if s + 1 < n_sub:
                    stages.append((s + 1, sub_tile(s + 1)))


def _in_proj(x2d, norm_g, w_in, pool_w3, pool_scale, mem_k, mem_v, q_norm_g, seq_len, mem_len):
    rows, d = x2d.shape
    pool_w = pool_w3.shape[0] * pool_w3.shape[1]
    mem_w = mem_k.shape[1]
    sb_w = (w_in.shape[1] - 2 * pool_w - 2 * mem_w) // 4
    assert seq_len % TM == 0 and rows % TM == 0
    tiles_per_seq = seq_len // TM
    row_spec = lambda width: pl.BlockSpec((TM, width), lambda i: (i, 0))
    full = lambda a: pl.BlockSpec(a.shape, lambda i: (0,) * a.ndim)
    mem_spec = pl.BlockSpec((mem_len, mem_w), lambda i: (i // tiles_per_seq, 0))
    out_sd = lambda width: jax.ShapeDtypeStruct((rows, width), BF16)
    return pl.pallas_call(
        functools.partial(_in_proj_kernel, seq_len, pool_w, sb_w, mem_w),
        out_shape=(out_sd(sb_w), out_sd(sb_w), out_sd(sb_w), out_sd(sb_w), out_sd(pool_w + mem_w)),
        grid=(rows // TM,),
        in_specs=[row_spec(d), full(norm_g), full(w_in), full(pool_w3), full(pool_scale),
                  mem_spec, mem_spec, full(q_norm_g)],
        out_specs=(row_spec(sb_w), row_spec(sb_w), row_spec(sb_w), row_spec(sb_w),
                   row_spec(pool_w + mem_w)),
        scratch_shapes=[pltpu.VMEM((TM // TM_SUB, MAX_WINDOW + TM_SUB, pool_w), F32),
                        pltpu.VMEM((MAX_WINDOW, pool_w), F32)],
        compiler_params=pltpu.CompilerParams(dimension_semantics=("arbitrary",),
                                             vmem_limit_bytes=VMEM_LIMIT),
        name="in_proj",
    )(x2d, norm_g, w_in, pool_w3, pool_scale, mem_k, mem_v, q_norm_g)


def _sb_out_kernel(n_heads, pool_w,
                   q_ref, k_ref, v_ref, gate_ref, pm_ref, x_ref, w_ref,
                   o_ref, tri_ref, qm_ref, vm_ref, acc_ref, carry_ref, mixed_ref):
    tq = tri_ref.shape[0]
    n_tiles = q_ref.shape[0] // tq
    n_pairs = n_heads // HEADS_PER_LANE_TILE
    first_block = pl.program_id(1) * n_tiles
    half = lax.broadcasted_iota(jnp.int32, (1, LANES), 1) // HEAD_DIM

    @pl.when(pl.program_id(1) == 0)
    def _():
        lane_head = (lax.broadcasted_iota(jnp.int32, (tq, v_ref.shape[1]), 1) // HEAD_DIM) % PV_GROUP
        for kb in range(v_ref.shape[0] // tq):
            v = v_ref[kb * tq:(kb + 1) * tq, :].astype(F32)
            for g in range(PV_GROUP):
                vm_ref[kb, g * tq:(g + 1) * tq, :] = jnp.where(lane_head == g, v, 0.0).astype(BF16)

    for t in range(n_tiles):
        for p in range(n_pairs):
            qp = q_ref[t * tq:(t + 1) * tq, p * LANES:(p + 1) * LANES].astype(F32)
            for hh in range(HEADS_PER_LANE_TILE):
                qm_ref[t, p * HEADS_PER_LANE_TILE + hh] = jnp.where(half == hh, qp, 0.0).astype(BF16)

    row = lax.broadcasted_iota(jnp.int32, (tq, tq), 0)
    col = lax.broadcasted_iota(jnp.int32, (tq, tq), 1)
    causal = col < row
    tri_ref[...] = jnp.where(row >= col, -1.0, 0.0).astype(BF16)

    def emit(units):
        logit, suffix, a = {}, {}, {}

        def logits(u):
            t, hd, kb, _ = units[u]
            p = hd // HEADS_PER_LANE_TILE
            k0 = pl.multiple_of(kb * tq, tq)
            logit[u] = _dot_nt(qm_ref[t, hd], k_ref[pl.ds(k0, tq), p * LANES:(p + 1) * LANES])

        def suffix_sum(u):
            diagonal = units[u][3]
            z = logit[u].astype(BF16)
            sp = jnp.maximum(z, 0.0) + jnp.log(1.0 + jnp.exp(-jnp.abs(z)))
            if diagonal:
                sp = jnp.where(causal, sp, 0.0)
            suffix[u] = _dot(sp, tri_ref[...])

        def weights(u):
            t, hd, _, diagonal = units[u]
            log_a = logit.pop(u) + suffix[u]
            if not diagonal:
                log_a = log_a + carry_ref[t, hd]
            w = jnp.exp(log_a)
            if diagonal:
                w = jnp.where(causal, w, 0.0)
                carry_ref[t, hd] = suffix.pop(u)[:, 0:1]
            else:
                carry_ref[t, hd] = carry_ref[t, hd] + suffix.pop(u)[:, 0:1]
            a[u] = w.astype(BF16)

        def weighted_values(u_last):
            t, hd, kb, diagonal = units[u_last]
            grp = hd // PV_GROUP
            a_cat = jnp.concatenate([a.pop(u) for u in range(u_last - PV_GROUP + 1, u_last + 1)], axis=1)
            out = _dot(a_cat, vm_ref[kb, :, grp * PV_WIDTH:(grp + 1) * PV_WIDTH])
            if diagonal:
                acc_ref[t, grp] = out
            else:
                acc_ref[t, grp] = acc_ref[t, grp] + out

        n = len(units)
        for step in range(n + 2):
            if step < n:
                logits(step)
            if 0 <= step - 1 < n:
                suffix_sum(step - 1)
            if 0 <= step - 2 < n:
                weights(step - 2)
                if units[step - 2][1] % PV_GROUP == PV_GROUP - 1:
                    weighted_values(step - 2)

    def head_units(t, kb, diagonal):
        return [(t, hd, kb, diagonal) for hd in range(n_heads)]

    units = []
    for depth in range(n_tiles):
        for t in range(depth, n_tiles):
            units += head_units(t, first_block + t - depth, depth == 0)
    emit(units)

    def body(j, _):
        kb = first_block - 1 - j
        units = []
        for grp in range(n_heads // PV_GROUP):
            for t in range(n_tiles):
                units += [(t, grp * PV_GROUP + g, kb, False) for g in range(PV_GROUP)]
        emit(units)
        return 0

    lax.fori_loop(0, first_block, body, 0)

    for t in range(n_tiles):
        rs = slice(t * tq, (t + 1) * tq)
        for grp in range(n_heads // PV_GROUP):
            sl = slice(grp * PV_WIDTH, (grp + 1) * PV_WIDTH)
            mixed_ref[rs, sl] = (acc_ref[t, grp] * gate_ref[rs, sl].astype(F32)).astype(BF16)
    sb_w = mixed_ref.shape[1]
    o_ref[...] = (x_ref[...] + _dot(mixed_ref[...], w_ref[pool_w:pool_w + sb_w, :])
                  + _dot(pm_ref[:, :pool_w], w_ref[:pool_w, :])
                  + _dot(pm_ref[:, pool_w:], w_ref[pool_w + sb_w:, :]))


def _sb_out(q, k, v, gate, pm, x2d, w_out, pool_w, batch, seq_len):
    rows, sb_w = q.shape
    d = x2d.shape[1]
    n_heads = sb_w // HEAD_DIM
    step_rows = TQ * Q_TILES
    assert seq_len % step_rows == 0
    steps = seq_len // step_rows
    k3 = k.reshape(batch, seq_len, sb_w)
    v3 = v.reshape(batch, seq_len, sb_w)
    row_spec = lambda width: pl.BlockSpec((step_rows, width), lambda b, i: (b * steps + i, 0))
    seq_spec = pl.BlockSpec((None, seq_len, sb_w), lambda b, i: (b, 0, 0))
    full = lambda a: pl.BlockSpec(a.shape, lambda b, i: (0, 0))
    return pl.pallas_call(
        functools.partial(_sb_out_kernel, n_heads, pool_w),
        out_shape=jax.ShapeDtypeStruct((rows, d), F32),
        grid=(batch, steps),
        in_specs=[row_spec(sb_w), seq_spec, seq_spec, row_spec(sb_w), row_spec(pm.shape[1]),
                  row_spec(d), full(w_out)],
        out_specs=row_spec(d),
        scratch_shapes=[pltpu.VMEM((TQ, TQ), BF16),
                        pltpu.VMEM((Q_TILES, n_heads, TQ, LANES), BF16),
                        pltpu.VMEM((seq_len // TQ, PV_GROUP * TQ, sb_w), BF16),
                        pltpu.VMEM((Q_TILES, n_heads // PV_GROUP, TQ, PV_WIDTH), F32),
                        pltpu.VMEM((Q_TILES, n_heads, TQ, 1), F32),
                        pltpu.VMEM((step_rows, sb_w), BF16)],
        compiler_params=pltpu.CompilerParams(dimension_semantics=("arbitrary", "arbitrary"),
                                             vmem_limit_bytes=VMEM_LIMIT),
        name="sb_out",
    )(q, k3, v3, gate, pm, x2d, w_out)


def kernel(x, mem, norm_g, w_in, pool_w, pool_scale, mem_norm_g, w_mem_kv, q_norm_g, k_norm_g, w_out):
    batch, seq_len, d = x.shape
    mem_len = mem.shape[1]
    depth = norm_g.shape[0]
    pool_width = pool_scale.shape[1]
    assert seq_len % TQ == 0
    x2d = x.reshape(batch * seq_len, d)
    for l in range(depth):
        mem_k, mem_v = _mem_kv(mem.reshape(batch * mem_len, d), mem_norm_g[l][None],
                               w_mem_kv[l].astype(BF16), k_norm_g[l][None])
        q, k, v, gate, pm = _in_proj(x2d, norm_g[l][None], w_in[l].astype(BF16), pool_w[l],
                                     pool_scale[l][None], mem_k, mem_v, q_norm_g[l][None],
                                     seq_len, mem_len)
        x2d = _sb_out(q, k, v, gate, pm, x2d, w_out[l].astype(BF16), pool_width, batch, seq_len)
    return x2d.reshape(batch, seq_len, d)
```

```python
import functools

import jax
import jax.numpy as jnp
from jax import lax
from jax.experimental import pallas as pl
from jax.experimental.pallas import tpu as pltpu

F32 = jnp.float32
BF16 = jnp.bfloat16

EPS = 1e-6
HEAD_DIM = 64
POOL_WINDOWS = (2, 4, 8, 16)
MAX_WINDOW = max(POOL_WINDOWS)
MEM_HEADS = 4
LANES = 128
HEADS_PER_LANE_TILE = LANES // HEAD_DIM
PV_GROUP = HEADS_PER_LANE_TILE
PV_WIDTH = PV_GROUP * HEAD_DIM

TM = 512
TM_SUB = 256
TQ = 256
Q_TILES = 4
VMEM_LIMIT = 56 * 1024 * 1024


def _silu(g):
    return g / (1.0 + jnp.exp(-g))


def _dot(a, b):
    return jnp.dot(a, b, preferred_element_type=F32)


def _dot_nt(a, b):
    return lax.dot_general(a, b, (((1,), (1,)), ((), ())), preferred_element_type=F32)


def _per_head(g):
    return jnp.concatenate([g] * MEM_HEADS, axis=1)


def _block_diag(w_ref):
    g, c, _ = w_ref.shape
    zero = jnp.zeros((c, c), F32)
    rows = [jnp.concatenate([w_ref[i] if j == i else zero for j in range(g)], axis=1) for i in range(g)]
    return jnp.concatenate(rows, axis=0).astype(BF16)


def _segment_mean_matrix(width, seg):
    r = lax.broadcasted_iota(jnp.int32, (width, width), 0) // seg
    c = lax.broadcasted_iota(jnp.int32, (width, width), 1) // seg
    return jnp.where(r == c, 1.0 / seg, 0.0).astype(BF16)


def _mem_kv_kernel(mem_ref, g_ref, w_ref, kg_ref, w_in_ref, w_out_ref,
                   k_out, v_out, w_in_out, w_out_out):
    w_in_out[...] = w_in_ref[...].astype(BF16)
    w_out_out[...] = w_out_ref[...].astype(BF16)

    m = mem_ref[...]
    ms = jnp.mean(m * m, axis=-1, keepdims=True)
    h = (m * lax.rsqrt(ms + EPS) * g_ref[...]).astype(BF16)
    kv = _dot(h, w_ref[...].astype(BF16))
    width = kv.shape[1] // 2
    k = kv[:, :width]
    v = kv[:, width:]
    seg = _segment_mean_matrix(width, width // MEM_HEADS)
    kms = _dot((k * k).astype(BF16), seg)
    k_out[...] = (k * lax.rsqrt(kms + EPS) * _per_head(kg_ref[...])).astype(BF16)
    v_out[...] = v.astype(BF16)


def _mem_kv(mem2d, mem_norm_g, w_mem_kv, k_norm_g, w_in, w_out):
    rows, d = mem2d.shape
    width = w_mem_kv.shape[1] // 2
    steps = 4
    assert rows % steps == 0 and w_in.shape[0] % steps == 0 and w_out.shape[0] % steps == 0
    chunk = lambda a: pl.BlockSpec((a.shape[0] // steps, a.shape[1]), lambda i: (i, 0))
    full = lambda a: pl.BlockSpec(a.shape, lambda i: (0, 0))
    kv_sd = jax.ShapeDtypeStruct((rows, width), BF16)
    kv_spec = pl.BlockSpec((rows // steps, width), lambda i: (i, 0))
    return pl.pallas_call(
        _mem_kv_kernel,
        out_shape=(kv_sd, kv_sd, jax.ShapeDtypeStruct(w_in.shape, BF16),
                   jax.ShapeDtypeStruct(w_out.shape, BF16)),
        grid=(steps,),
        in_specs=[chunk(mem2d), full(mem_norm_g), full(w_mem_kv), full(k_norm_g),
                  chunk(w_in), chunk(w_out)],
        out_specs=(kv_spec, kv_spec, chunk(w_in), chunk(w_out)),
        compiler_params=pltpu.CompilerParams(dimension_semantics=("arbitrary",),
                                             vmem_limit_bytes=VMEM_LIMIT),
        name="mem_kv",
    )(mem2d, mem_norm_g, w_mem_kv, k_norm_g, w_in, w_out)


DONE, MIX_READY = object(), object()

def _in_proj_kernel(seq_len, pool_w, sb_w, mem_w,
                    x_ref, g_ref, w_ref, pw_ref, pscale_ref, mk_ref, mv_ref, qg_ref,
                    q_out, k_out, v_out, gate_out, pm_out, pbuf, halo_ref):
    tm = x_ref.shape[0]
    n_sub = pbuf.shape[0]
    rows = tm // n_sub
    t0 = (pl.program_id(0) * tm) % seq_len
    head_w = mem_w // MEM_HEADS

    @pl.when(t0 == 0)
    def _():
        halo_ref[...] = jnp.zeros(halo_ref.shape, F32)

    lane = lax.broadcasted_iota(jnp.int32, (rows, pool_w), 1)
    group = lane // (pool_w // len(POOL_WINDOWS))
    row_iota = lax.broadcasted_iota(jnp.int32, (rows, pool_w), 0)
    seg = _segment_mean_matrix(mem_w, head_w)
    q_gain = _per_head(qg_ref[...])
    pool_mix = _block_diag(pw_ref)
    lane_q = lax.broadcasted_iota(jnp.int32, (rows, mem_w), 1) // head_w
    mv = mv_ref[...].astype(F32)
    lane_v = lax.broadcasted_iota(jnp.int32, mv.shape, 1) // head_w
    mv_heads = [jnp.where(lane_v == hd, mv, 0.0).astype(BF16) for hd in range(MEM_HEADS)]

    h, kept = {}, {}

    def normalize(s, n_chunks=4):
        chunk = rows // n_chunks
        parts = []
        for i in range(n_chunks):
            x = x_ref[s * rows + i * chunk:s * rows + (i + 1) * chunk, :]
            ms = jnp.mean(x * x, axis=-1, keepdims=True)
            parts.append((x * lax.rsqrt(ms + EPS) * g_ref[...]).astype(BF16))
            if i + 1 < n_chunks:
                yield
        h[s] = jnp.concatenate(parts, axis=0)

    def project(s):
        hs = h.pop(s)
        rs = slice(s * rows, (s + 1) * rows)

        def proj(c0, width):
            return _dot(hs, w_ref[:, c0:c0 + width])

        c_pool, c_sb, c_mem = 0, 2 * pool_w, 2 * pool_w + 4 * sb_w
        pool_v = proj(c_pool, pool_w)
        yield
        pool_g = proj(c_pool + pool_w, pool_w)
        yield
        mem_q = proj(c_mem, mem_w)
        yield
        mem_g = proj(c_mem + mem_w, mem_w)
        kept[s] = (pool_v, pool_g, mem_q, mem_g)
        yield MIX_READY
        q_out[rs, :] = (proj(c_sb, sb_w) * HEAD_DIM ** -0.5).astype(BF16)
        yield
        k_out[rs, :] = proj(c_sb + sb_w, sb_w).astype(BF16)
        yield
        v_out[rs, :] = proj(c_sb + 2 * sb_w, sb_w).astype(BF16)
        yield
        gate_out[rs, :] = _silu(proj(c_sb + 3 * sb_w, sb_w)).astype(BF16)

    def mix(s):
        pool_v, pool_g, mem_q, mem_g = kept.pop(s)
        rs = slice(s * rows, (s + 1) * rows)

        pbuf[s, 0:MAX_WINDOW, :] = halo_ref[...]
        pbuf[s, MAX_WINDOW:MAX_WINDOW + rows, :] = pool_v
        halo_ref[...] = pool_v[rows - MAX_WINDOW:, :]
        t_seq = t0 + s * rows + row_iota
        win_sum = pool_v
        window = jnp.full((rows, pool_w), 1, jnp.int32)
        acc = pool_v
        shift = 1
        for g, w in enumerate(POOL_WINDOWS):
            while shift < w:
                acc = acc + pbuf[s, MAX_WINDOW - shift:MAX_WINDOW - shift + rows, :]
                shift += 1
            win_sum = jnp.where(group == g, acc, win_sum)
            window = jnp.where(group == g, w, window)
        count = jnp.minimum(t_seq + 1, window).astype(F32)
        pooled = (win_sum / count - pool_v).astype(BF16)
        y_pool = _dot(pooled, pool_mix)
        yield

        qms = _dot((mem_q * mem_q).astype(BF16), seg)
        yield
        pm_out[rs, 0:pool_w] = (y_pool * pscale_ref[...] * _silu(pool_g)).astype(BF16)
        qn = mem_q * lax.rsqrt(qms + EPS) * q_gain
        qn = qn * (head_w ** -0.5)

        scores = []
        for hd in range(MEM_HEADS):
            scores.append(_dot_nt(jnp.where(lane_q == hd, qn, 0.0).astype(BF16), mk_ref[...]))
            yield
        y_mem = None
        for hd in range(MEM_HEADS):
            e = jnp.exp(scores[hd] - jnp.max(scores[hd], axis=-1, keepdims=True))
            pv = _dot(e.astype(BF16), mv_heads[hd])
            yield
            pv = pv * (1.0 / jnp.sum(e, axis=-1, keepdims=True))
            y_mem = pv if y_mem is None else y_mem + pv
        pm_out[rs, pool_w:pool_w + mem_w] = (y_mem * _silu(mem_g)).astype(BF16)

    def sub_tile(s):
        yield from normalize(s)
        yield from project(s)

    stages = [(0, sub_tile(0))]
    while stages:
        for entry in list(stages):
            s, stage = entry
            event = next(stage, DONE)
            if event is DONE:
                stages.remove(entry)
            elif event is MIX_READY:
                stages.append((s, mix(s)))
                if s + 1 < n_sub:
                    stages.append((s + 1, sub_tile(s + 1)))


def _in_proj(x2d, norm_g, w_in, pool_w3, pool_scale, mem_k, mem_v, q_norm_g, seq_len, mem_len):
    rows, d = x2d.shape
    pool_w = pool_w3.shape[0] * pool_w3.shape[1]
    mem_w = mem_k.shape[1]
    sb_w = (w_in.shape[1] - 2 * pool_w - 2 * mem_w) // 4
    assert seq_len % TM == 0 and rows % TM == 0
    tiles_per_seq = seq_len // TM
    row_spec = lambda width: pl.BlockSpec((TM, width), lambda i: (i, 0))
    full = lambda a: pl.BlockSpec(a.shape, lambda i: (0,) * a.ndim)
    mem_spec = pl.BlockSpec((mem_len, mem_w), lambda i: (i // tiles_per_seq, 0))
    out_sd = lambda width: jax.ShapeDtypeStruct((rows, width), BF16)
    return pl.pallas_call(
        functools.partial(_in_proj_kernel, seq_len, pool_w, sb_w, mem_w),
        out_shape=(out_sd(sb_w), out_sd(sb_w), out_sd(sb_w), out_sd(sb_w), out_sd(pool_w + mem_w)),
        grid=(rows // TM,),
        in_specs=[row_spec(d), full(norm_g), full(w_in), full(pool_w3), full(pool_scale),
                  mem_spec, mem_spec, full(q_norm_g)],
        out_specs=(row_spec(sb_w), row_spec(sb_w), row_spec(sb_w), row_spec(sb_w),
                   row_spec(pool_w + mem_w)),
        scratch_shapes=[pltpu.VMEM((TM // TM_SUB, MAX_WINDOW + TM_SUB, pool_w), F32),
                        pltpu.VMEM((MAX_WINDOW, pool_w), F32)],
        compiler_params=pltpu.CompilerParams(dimension_semantics=("arbitrary",),
                                             vmem_limit_bytes=VMEM_LIMIT),
        name="in_proj",
    )(x2d, norm_g, w_in, pool_w3, pool_scale, mem_k, mem_v, q_norm_g)


def _sb_out_kernel(n_heads, pool_w,
                   q_ref, k_ref, v_ref, gate_ref, pm_ref, x_ref, w_ref,
                   o_ref, tri_ref, qm_ref, vm_ref, acc_ref, carry_ref, mixed_ref):
    tq = tri_ref.shape[0]
    n_tiles = q_ref.shape[0] // tq
    n_pairs = n_heads // HEADS_PER_LANE_TILE
    first_block = pl.program_id(1) * n_tiles
    half = lax.broadcasted_iota(jnp.int32, (1, LANES), 1) // HEAD_DIM

    @pl.when(pl.program_id(1) == 0)
    def _():
        lane_head = (lax.broadcasted_iota(jnp.int32, (tq, v_ref.shape[1]), 1) // HEAD_DIM) % PV_GROUP
        for kb in range(v_ref.shape[0] // tq):
            v = v_ref[kb * tq:(kb + 1) * tq, :].astype(F32)
            for g in range(PV_GROUP):
                vm_ref[kb, g * tq:(g + 1) * tq, :] = jnp.where(lane_head == g, v, 0.0).astype(BF16)

    for t in range(n_tiles):
        for p in range(n_pairs):
            qp = q_ref[t * tq:(t + 1) * tq, p * LANES:(p + 1) * LANES].astype(F32)
            for hh in range(HEADS_PER_LANE_TILE):
                qm_ref[t, p * HEADS_PER_LANE_TILE + hh] = jnp.where(half == hh, qp, 0.0).astype(BF16)

    row = lax.broadcasted_iota(jnp.int32, (tq, tq), 0)
    col = lax.broadcasted_iota(jnp.int32, (tq, tq), 1)
    causal = col < row
    tri_ref[...] = jnp.where(row >= col, -1.0, 0.0).astype(BF16)

    def emit(units):
        logit, suffix, a = {}, {}, {}

        def logits(u):
            t, hd, kb, _ = units[u]
            p = hd // HEADS_PER_LANE_TILE
            k0 = pl.multiple_of(kb * tq, tq)
            logit[u] = _dot_nt(qm_ref[t, hd], k_ref[pl.ds(k0, tq), p * LANES:(p + 1) * LANES])

        def suffix_sum(u):
            diagonal = units[u][3]
            z = logit[u].astype(BF16)
            sp = jnp.maximum(z, 0.0) + jnp.log(1.0 + jnp.exp(-jnp.abs(z)))
            if diagonal:
                sp = jnp.where(causal, sp, 0.0)
            suffix[u] = _dot(sp, tri_ref[...])

        def weights(u):
            t, hd, _, diagonal = units[u]
            log_a = logit.pop(u) + suffix[u]
            if not diagonal:
                log_a = log_a + carry_ref[t, hd]
            w = jnp.exp(log_a)
            if diagonal:
                w = jnp.where(causal, w, 0.0)
                carry_ref[t, hd] = suffix.pop(u)[:, 0:1]
            else:
                carry_ref[t, hd] = carry_ref[t, hd] + suffix.pop(u)[:, 0:1]
            a[u] = w.astype(BF16)

        def weighted_values(u_last):
            t, hd, kb, diagonal = units[u_last]
            grp = hd // PV_GROUP
            a_cat = jnp.concatenate([a.pop(u) for u in range(u_last - PV_GROUP + 1, u_last + 1)], axis=1)
            out = _dot(a_cat, vm_ref[kb, :, grp * PV_WIDTH:(grp + 1) * PV_WIDTH])
            if diagonal:
                acc_ref[t, grp] = out
            else:
                acc_ref[t, grp] = acc_ref[t, grp] + out

        n = len(units)
        for step in range(n + 2):
            if step < n:
                logits(step)
            if 0 <= step - 1 < n:
                suffix_sum(step - 1)
            if 0 <= step - 2 < n:
                weights(step - 2)
                if units[step - 2][1] % PV_GROUP == PV_GROUP - 1:
                    weighted_values(step - 2)

    def head_units(t, kb, diagonal):
        return [(t, hd, kb, diagonal) for hd in range(n_heads)]

    units = []
    for depth in range(n_tiles):
        for t in range(depth, n_tiles):
            units += head_units(t, first_block + t - depth, depth == 0)
    emit(units)

    def body(j, _):
        units = []
        for i in range(n_tiles):
            kb = first_block - 1 - (j * n_tiles + i)
            for grp in range(n_heads // PV_GROUP):
                for t in range(n_tiles):
                    units += [(t, grp * PV_GROUP + g, kb, False) for g in range(PV_GROUP)]
        emit(units)
        return 0

    lax.fori_loop(0, pl.program_id(1), body, 0)

    for t in range(n_tiles):
        rs = slice(t * tq, (t + 1) * tq)
        for grp in range(n_heads // PV_GROUP):
            sl = slice(grp * PV_WIDTH, (grp + 1) * PV_WIDTH)
            mixed_ref[rs, sl] = (acc_ref[t, grp] * gate_ref[rs, sl].astype(F32)).astype(BF16)
    sb_w = mixed_ref.shape[1]
    o_ref[...] = (x_ref[...] + _dot(mixed_ref[...], w_ref[pool_w:pool_w + sb_w, :])
                  + _dot(pm_ref[:, :pool_w], w_ref[:pool_w, :])
                  + _dot(pm_ref[:, pool_w:], w_ref[pool_w + sb_w:, :]))


def _sb_out(q, k, v, gate, pm, x2d, w_out, pool_w, batch, seq_len):
    rows, sb_w = q.shape
    d = x2d.shape[1]
    n_heads = sb_w // HEAD_DIM
    step_rows = TQ * Q_TILES
    assert seq_len % step_rows == 0
    steps = seq_len // step_rows
    k3 = k.reshape(batch, seq_len, sb_w)
    v3 = v.reshape(batch, seq_len, sb_w)
    row_spec = lambda width: pl.BlockSpec((step_rows, width), lambda b, i: (b * steps + i, 0))
    seq_spec = pl.BlockSpec((None, seq_len, sb_w), lambda b, i: (b, 0, 0))
    full = lambda a: pl.BlockSpec(a.shape, lambda b, i: (0, 0))
    return pl.pallas_call(
        functools.partial(_sb_out_kernel, n_heads, pool_w),
        out_shape=jax.ShapeDtypeStruct((rows, d), F32),
        grid=(batch, steps),
        in_specs=[row_spec(sb_w), seq_spec, seq_spec, row_spec(sb_w), row_spec(pm.shape[1]),
                  row_spec(d), full(w_out)],
        out_specs=row_spec(d),
        scratch_shapes=[pltpu.VMEM((TQ, TQ), BF16),
                        pltpu.VMEM((Q_TILES, n_heads, TQ, LANES), BF16),
                        pltpu.VMEM((seq_len // TQ, PV_GROUP * TQ, sb_w), BF16),
                        pltpu.VMEM((Q_TILES, n_heads // PV_GROUP, TQ, PV_WIDTH), F32),
                        pltpu.VMEM((Q_TILES, n_heads, TQ, 1), F32),
                        pltpu.VMEM((step_rows, sb_w), BF16)],
        compiler_params=pltpu.CompilerParams(dimension_semantics=("arbitrary", "arbitrary"),
                                             vmem_limit_bytes=VMEM_LIMIT),
        name="sb_out",
    )(q, k3, v3, gate, pm, x2d, w_out)


def kernel(x, mem, norm_g, w_in, pool_w, pool_scale, mem_norm_g, w_mem_kv, q_norm_g, k_norm_g, w_out):
    batch, seq_len, d = x.shape
    mem_len = mem.shape[1]
    depth = norm_g.shape[0]
    pool_width = pool_scale.shape[1]
    assert seq_len % TQ == 0
    x2d = x.reshape(batch * seq_len, d)
    for l in range(depth):
        mem_k, mem_v, w_in_b, w_out_b = _mem_kv(mem.reshape(batch * mem_len, d), mem_norm_g[l][None],
                                                w_mem_kv[l], k_norm_g[l][None], w_in[l], w_out[l])
        q, k, v, gate, pm = _in_proj(x2d, norm_g[l][None], w_in_b, pool_w[l],
                                     pool_scale[l][None], mem_k, mem_v, q_norm_g[l][None],
                                     seq_len, mem_len)
        x2d = _sb_out(q, k, v, gate, pm, x2d, w_out_b, pool_width, batch, seq_len)
    return x2d.reshape(batch, seq_len, d)
```

```python
import functools

import jax
import jax.numpy as jnp
from jax import lax
from jax.experimental import pallas as pl
from jax.experimental.pallas import tpu as pltpu

F32 = jnp.float32
BF16 = jnp.bfloat16

EPS = 1e-6
HEAD_DIM = 64
POOL_WINDOWS = (2, 4, 8, 16)
MAX_WINDOW = max(POOL_WINDOWS)
MEM_HEADS = 4
LANES = 128
HEADS_PER_LANE_TILE = LANES // HEAD_DIM
PV_GROUP = HEADS_PER_LANE_TILE
PV_WIDTH = PV_GROUP * HEAD_DIM

TM = 512
TM_SUB = 256
TQ = 256
Q_TILES = 4
VMEM_LIMIT = 56 * 1024 * 1024


def _silu(g):
    return g / (1.0 + jnp.exp(-g))


def _dot(a, b):
    return jnp.dot(a, b, preferred_element_type=F32)


def _dot_nt(a, b):
    return lax.dot_general(a, b, (((1,), (1,)), ((), ())), preferred_element_type=F32)


def _per_head(g):
    return jnp.concatenate([g] * MEM_HEADS, axis=1)


def _block_diag(w_ref):
    g, c, _ = w_ref.shape
    zero = jnp.zeros((c, c), F32)
    rows = [jnp.concatenate([w_ref[i] if j == i else zero for j in range(g)], axis=1) for i in range(g)]
    return jnp.concatenate(rows, axis=0).astype(BF16)


def _segment_mean_matrix(width, seg):
    r = lax.broadcasted_iota(jnp.int32, (width, width), 0) // seg
    c = lax.broadcasted_iota(jnp.int32, (width, width), 1) // seg
    return jnp.where(r == c, 1.0 / seg, 0.0).astype(BF16)


def _mem_kv_kernel(mem_ref, g_ref, w_ref, kg_ref, w_in_ref, w_out_ref,
                   k_out, v_out, w_in_out, w_out_out):
    w_in_out[...] = w_in_ref[...].astype(BF16)
    w_out_out[...] = w_out_ref[...].astype(BF16)

    m = mem_ref[...]
    ms = jnp.mean(m * m, axis=-1, keepdims=True)
    h = (m * lax.rsqrt(ms + EPS) * g_ref[...]).astype(BF16)
    kv = _dot(h, w_ref[...].astype(BF16))
    width = kv.shape[1] // 2
    k = kv[:, :width]
    v = kv[:, width:]
    seg = _segment_mean_matrix(width, width // MEM_HEADS)
    kms = _dot((k * k).astype(BF16), seg)
    k_out[...] = (k * lax.rsqrt(kms + EPS) * _per_head(kg_ref[...])).astype(BF16)
    v_out[...] = v.astype(BF16)


def _mem_kv(mem2d, mem_norm_g, w_mem_kv, k_norm_g, w_in, w_out):
    rows, d = mem2d.shape
    width = w_mem_kv.shape[1] // 2
    steps = 4
    assert rows % steps == 0 and w_in.shape[0] % steps == 0 and w_out.shape[0] % steps == 0
    chunk = lambda a: pl.BlockSpec((a.shape[0] // steps, a.shape[1]), lambda i: (i, 0))
    full = lambda a: pl.BlockSpec(a.shape, lambda i: (0, 0))
    kv_sd = jax.ShapeDtypeStruct((rows, width), BF16)
    kv_spec = pl.BlockSpec((rows // steps, width), lambda i: (i, 0))
    return pl.pallas_call(
        _mem_kv_kernel,
        out_shape=(kv_sd, kv_sd, jax.ShapeDtypeStruct(w_in.shape, BF16),
                   jax.ShapeDtypeStruct(w_out.shape, BF16)),
        grid=(steps,),
        in_specs=[chunk(mem2d), full(mem_norm_g), full(w_mem_kv), full(k_norm_g),
                  chunk(w_in), chunk(w_out)],
        out_specs=(kv_spec, kv_spec, chunk(w_in), chunk(w_out)),
        compiler_params=pltpu.CompilerParams(dimension_semantics=("arbitrary",),
                                             vmem_limit_bytes=VMEM_LIMIT),
        name="mem_kv",
    )(mem2d, mem_norm_g, w_mem_kv, k_norm_g, w_in, w_out)


DONE, MIX_READY = object(), object()

def _in_proj_kernel(seq_len, pool_w, sb_w, mem_w,
                    x_ref, g_ref, w_ref, pw_ref, pscale_ref, mk_ref, mv_ref, qg_ref,
                    q_out, k_out, v_out, gate_out, pm_out, pbuf, halo_ref):
    tm = x_ref.shape[0]
    n_sub = pbuf.shape[0]
    rows = tm // n_sub
    t0 = (pl.program_id(0) * tm) % seq_len
    head_w = mem_w // MEM_HEADS

    @pl.when(t0 == 0)
    def _():
        halo_ref[...] = jnp.zeros(halo_ref.shape, F32)

    lane = lax.broadcasted_iota(jnp.int32, (rows, pool_w), 1)
    group = lane // (pool_w // len(POOL_WINDOWS))
    row_iota = lax.broadcasted_iota(jnp.int32, (rows, pool_w), 0)
    seg = _segment_mean_matrix(mem_w, head_w)
    q_gain = _per_head(qg_ref[...])
    pool_mix = _block_diag(pw_ref)
    lane_q = lax.broadcasted_iota(jnp.int32, (rows, mem_w), 1) // head_w
    mv = mv_ref[...].astype(F32)
    lane_v = lax.broadcasted_iota(jnp.int32, mv.shape, 1) // head_w
    mv_heads = [jnp.where(lane_v == hd, mv, 0.0).astype(BF16) for hd in range(MEM_HEADS)]

    h, kept = {}, {}

    def normalize(s, n_chunks=4):
        chunk = rows // n_chunks
        parts = []
        for i in range(n_chunks):
            x = x_ref[s * rows + i * chunk:s * rows + (i + 1) * chunk, :]
            ms = jnp.mean(x * x, axis=-1, keepdims=True)
            parts.append((x * lax.rsqrt(ms + EPS) * g_ref[...]).astype(BF16))
            if i + 1 < n_chunks:
                yield
        h[s] = jnp.concatenate(parts, axis=0)

    def project(s):
        hs = h.pop(s)
        rs = slice(s * rows, (s + 1) * rows)

        def proj(c0, width):
            return _dot(hs, w_ref[:, c0:c0 + width])

        c_pool, c_sb, c_mem = 0, 2 * pool_w, 2 * pool_w + 4 * sb_w
        pool_v = proj(c_pool, pool_w)
        yield
        pool_g = proj(c_pool + pool_w, pool_w)
        yield
        mem_q = proj(c_mem, mem_w)
        yield
        mem_g = proj(c_mem + mem_w, mem_w)
        kept[s] = (pool_v, pool_g, mem_q, mem_g)
        yield MIX_READY
        q_out[rs, :] = (proj(c_sb, sb_w) * HEAD_DIM ** -0.5).astype(BF16)
        yield
        k_out[rs, :] = proj(c_sb + sb_w, sb_w).astype(BF16)
        yield
        v_out[rs, :] = proj(c_sb + 2 * sb_w, sb_w).astype(BF16)
        yield
        gate_out[rs, :] = _silu(proj(c_sb + 3 * sb_w, sb_w)).astype(BF16)

    def mix(s):
        pool_v, pool_g, mem_q, mem_g = kept.pop(s)
        rs = slice(s * rows, (s + 1) * rows)

        pbuf[s, 0:MAX_WINDOW, :] = halo_ref[...]
        pbuf[s, MAX_WINDOW:MAX_WINDOW + rows, :] = pool_v
        halo_ref[...] = pool_v[rows - MAX_WINDOW:, :]
        t_seq = t0 + s * rows + row_iota
        win_sum = pool_v
        window = jnp.full((rows, pool_w), 1, jnp.int32)
        acc = pool_v
        shift = 1
        for g, w in enumerate(POOL_WINDOWS):
            while shift < w:
                acc = acc + pbuf[s, MAX_WINDOW - shift:MAX_WINDOW - shift + rows, :]
                shift += 1
            win_sum = jnp.where(group == g, acc, win_sum)
            window = jnp.where(group == g, w, window)
        count = jnp.minimum(t_seq + 1, window).astype(F32)
        pooled = (win_sum / count - pool_v).astype(BF16)
        y_pool = _dot(pooled, pool_mix)
        yield

        qms = _dot((mem_q * mem_q).astype(BF16), seg)
        yield
        pm_out[rs, 0:pool_w] = (y_pool * pscale_ref[...] * _silu(pool_g)).astype(BF16)
        qn = mem_q * lax.rsqrt(qms + EPS) * q_gain
        qn = qn * (head_w ** -0.5)

        scores = []
        for hd in range(MEM_HEADS):
            scores.append(_dot_nt(jnp.where(lane_q == hd, qn, 0.0).astype(BF16), mk_ref[...]))
            yield
        y_mem = None
        for hd in range(MEM_HEADS):
            e = jnp.exp(scores[hd] - jnp.max(scores[hd], axis=-1, keepdims=True))
            pv = _dot(e.astype(BF16), mv_heads[hd])
            yield
            pv = pv * (1.0 / jnp.sum(e, axis=-1, keepdims=True))
            y_mem = pv if y_mem is None else y_mem + pv
        pm_out[rs, pool_w:pool_w + mem_w] = (y_mem * _silu(mem_g)).astype(BF16)

    def sub_tile(s):
        yield from normalize(s)
        yield from project(s)

    stages = [(0, sub_tile(0))]
    while stages:
        for entry in list(stages):
            s, stage = entry
            event = next(stage, DONE)
            if event is DONE:
                stages.remove(entry)
            elif event is MIX_READY:
                stages.append((s, mix(s)))
                if s + 1 < n_sub:
                    stages.append((s + 1, sub_tile(s + 1)))


def _in_proj(x2d, norm_g, w_in, pool_w3, pool_scale, mem_k, mem_v, q_norm_g, seq_len, mem_len):
    rows, d = x2d.shape
    pool_w = pool_w3.shape[0] * pool_w3.shape[1]
    mem_w = mem_k.shape[1]
    sb_w = (w_in.shape[1] - 2 * pool_w - 2 * mem_w) // 4
    assert seq_len % TM == 0 and rows % TM == 0
    tiles_per_seq = seq_len // TM
    row_spec = lambda width: pl.BlockSpec((TM, width), lambda i: (i, 0))
    full = lambda a: pl.BlockSpec(a.shape, lambda i: (0,) * a.ndim)
    mem_spec = pl.BlockSpec((mem_len, mem_w), lambda i: (i // tiles_per_seq, 0))
    out_sd = lambda width: jax.ShapeDtypeStruct((rows, width), BF16)
    return pl.pallas_call(
        functools.partial(_in_proj_kernel, seq_len, pool_w, sb_w, mem_w),
        out_shape=(out_sd(sb_w), out_sd(sb_w), out_sd(sb_w), out_sd(sb_w), out_sd(pool_w + mem_w)),
        grid=(rows // TM,),
        in_specs=[row_spec(d), full(norm_g), full(w_in), full(pool_w3), full(pool_scale),
                  mem_spec, mem_spec, full(q_norm_g)],
        out_specs=(row_spec(sb_w), row_spec(sb_w), row_spec(sb_w), row_spec(sb_w),
                   row_spec(pool_w + mem_w)),
        scratch_shapes=[pltpu.VMEM((TM // TM_SUB, MAX_WINDOW + TM_SUB, pool_w), F32),
                        pltpu.VMEM((MAX_WINDOW, pool_w), F32)],
        compiler_params=pltpu.CompilerParams(dimension_semantics=("arbitrary",),
                                             vmem_limit_bytes=VMEM_LIMIT),
        name="in_proj",
    )(x2d, norm_g, w_in, pool_w3, pool_scale, mem_k, mem_v, q_norm_g)


def _sb_out_kernel(n_heads, pool_w,
                   q_ref, k_ref, v_ref, gate_ref, pm_ref, x_ref, w_ref,
                   o_ref, tri_ref, qm_ref, vm_ref, acc_ref, carry_ref, mixed_ref):
    tq = tri_ref.shape[0]
    n_tiles = q_ref.shape[0] // tq
    n_pairs = n_heads // HEADS_PER_LANE_TILE
    first_block = pl.program_id(1) * n_tiles
    half = lax.broadcasted_iota(jnp.int32, (1, LANES), 1) // HEAD_DIM

    @pl.when(pl.program_id(1) == 0)
    def _():
        lane_head = (lax.broadcasted_iota(jnp.int32, (tq, v_ref.shape[1]), 1) // HEAD_DIM) % PV_GROUP
        for kb in range(v_ref.shape[0] // tq):
            v = v_ref[kb * tq:(kb + 1) * tq, :].astype(F32)
            for g in range(PV_GROUP):
                vm_ref[kb, g * tq:(g + 1) * tq, :] = jnp.where(lane_head == g, v, 0.0).astype(BF16)

    for t in range(n_tiles):
        for p in range(n_pairs):
            qp = q_ref[t * tq:(t + 1) * tq, p * LANES:(p + 1) * LANES].astype(F32)
            for hh in range(HEADS_PER_LANE_TILE):
                qm_ref[t, p * HEADS_PER_LANE_TILE + hh] = jnp.where(half == hh, qp, 0.0).astype(BF16)

    row = lax.broadcasted_iota(jnp.int32, (tq, tq), 0)
    col = lax.broadcasted_iota(jnp.int32, (tq, tq), 1)
    causal = col < row
    tri_ref[...] = jnp.where(row >= col, -1.0, 0.0).astype(BF16)

    def emit(units):
        logit, suffix, a = {}, {}, {}

        def logits(u):
            t, hd, kb, _ = units[u]
            p = hd // HEADS_PER_LANE_TILE
            k0 = pl.multiple_of(kb * tq, tq)
            logit[u] = _dot_nt(qm_ref[t, hd], k_ref[pl.ds(k0, tq), p * LANES:(p + 1) * LANES])

        def suffix_sum(u):
            diagonal = units[u][3]
            z = logit[u].astype(BF16)
            sp = jnp.maximum(z, 0.0) + jnp.log(1.0 + jnp.exp(-jnp.abs(z)))
            if diagonal:
                sp = jnp.where(causal, sp, 0.0)
            suffix[u] = _dot(sp, tri_ref[...])

        def weights(u):
            t, hd, _, diagonal = units[u]
            log_a = logit.pop(u) + suffix[u]
            if not diagonal:
                log_a = log_a + carry_ref[t, hd]
            w = jnp.exp(log_a)
            if diagonal:
                w = jnp.where(causal, w, 0.0)
                carry_ref[t, hd] = suffix.pop(u)[:, 0:1]
            else:
                carry_ref[t, hd] = carry_ref[t, hd] + suffix.pop(u)[:, 0:1]
            a[u] = w.astype(BF16)

        def weighted_values(u_last):
            t, hd, kb, diagonal = units[u_last]
            grp = hd // PV_GROUP
            a_cat = jnp.concatenate([a.pop(u) for u in range(u_last - PV_GROUP + 1, u_last + 1)], axis=1)
            out = _dot(a_cat, vm_ref[kb, :, grp * PV_WIDTH:(grp + 1) * PV_WIDTH])
            if diagonal:
                acc_ref[t, grp] = out
            else:
                acc_ref[t, grp] = acc_ref[t, grp] + out

        n = len(units)
        for step in range(n + 2):
            if step < n:
                logits(step)
            if 0 <= step - 1 < n:
                suffix_sum(step - 1)
            if 0 <= step - 2 < n:
                weights(step - 2)
                if units[step - 2][1] % PV_GROUP == PV_GROUP - 1:
                    weighted_values(step - 2)

    def head_units(t, kb, diagonal):
        return [(t, hd, kb, diagonal) for hd in range(n_heads)]

    units = []
    for depth in range(n_tiles):
        for t in range(depth, n_tiles):
            units += head_units(t, first_block + t - depth, depth == 0)
    emit(units)

    def body(j, _):
        kb = first_block - 1 - j
        units = []
        for grp in range(n_heads // PV_GROUP):
            for t in range(n_tiles):
                units += [(t, grp * PV_GROUP + g, kb, False) for g in range(PV_GROUP)]
        emit(units)
        return 0

    lax.fori_loop(0, first_block, body, 0)

    for t in range(n_tiles):
        rs = slice(t * tq, (t + 1) * tq)
        for grp in range(n_heads // PV_GROUP):
            sl = slice(grp * PV_WIDTH, (grp + 1) * PV_WIDTH)
            mixed_ref[rs, sl] = (acc_ref[t, grp] * gate_ref[rs, sl].astype(F32)).astype(BF16)
    sb_w = mixed_ref.shape[1]
    o_ref[...] = (x_ref[...] + _dot(mixed_ref[...], w_ref[pool_w:pool_w + sb_w, :])
                  + _dot(pm_ref[:, :pool_w], w_ref[:pool_w, :])
                  + _dot(pm_ref[:, pool_w:], w_ref[pool_w + sb_w:, :]))


def _sb_out(q, k, v, gate, pm, x2d, w_out, pool_w, batch, seq_len):
    rows, sb_w = q.shape
    d = x2d.shape[1]
    n_heads = sb_w // HEAD_DIM
    step_rows = TQ * Q_TILES
    assert seq_len % step_rows == 0
    steps = seq_len // step_rows
    k3 = k.reshape(batch, seq_len, sb_w)
    v3 = v.reshape(batch, seq_len, sb_w)
    row_spec = lambda width: pl.BlockSpec((step_rows, width), lambda b, i: (b * steps + i, 0))
    seq_spec = pl.BlockSpec((None, seq_len, sb_w), lambda b, i: (b, 0, 0))
    full = lambda a: pl.BlockSpec(a.shape, lambda b, i: (0, 0))
    return pl.pallas_call(
        functools.partial(_sb_out_kernel, n_heads, pool_w),
        out_shape=jax.ShapeDtypeStruct((rows, d), F32),
        grid=(batch, steps),
        in_specs=[row_spec(sb_w), seq_spec, seq_spec, row_spec(sb_w), row_spec(pm.shape[1]),
                  row_spec(d), full(w_out)],
        out_specs=row_spec(d),
        scratch_shapes=[pltpu.VMEM((TQ, TQ), BF16),
                        pltpu.VMEM((Q_TILES, n_heads, TQ, LANES), BF16),
                        pltpu.VMEM((seq_len // TQ, PV_GROUP * TQ, sb_w), BF16),
                        pltpu.VMEM((Q_TILES, n_heads // PV_GROUP, TQ, PV_WIDTH), F32),
                        pltpu.VMEM((Q_TILES, n_heads, TQ, 1), F32),
                        pltpu.VMEM((step_rows, sb_w), BF16)],
        compiler_params=pltpu.CompilerParams(dimension_semantics=("arbitrary", "arbitrary"),
                                             vmem_limit_bytes=VMEM_LIMIT),
        name="sb_out",
    )(q, k3, v3, gate, pm, x2d, w_out)


def kernel(x, mem, norm_g, w_in, pool_w, pool_scale, mem_norm_g, w_mem_kv, q_norm_g, k_norm_g, w_out):
    batch, seq_len, d = x.shape
    mem_len = mem.shape[1]
    depth = norm_g.shape[0]
    pool_width = pool_scale.shape[1]
    assert seq_len % TQ == 0
    x2d = x.reshape(batch * seq_len, d)
    for l in range(depth):
        mem_k, mem_v, w_in_b, w_out_b = _mem_kv(mem.reshape(batch * mem_len, d), mem_norm_g[l][None],
                                                w_mem_kv[l], k_norm_g[l][None], w_in[l], w_out[l])
        q, k, v, gate, pm = _in_proj(x2d, norm_g[l][None], w_in_b, pool_w[l],
                                     pool_scale[l][None], mem_k, mem_v, q_norm_g[l][None],
                                     seq_len, mem_len)
        x2d = _sb_out(q, k, v, gate, pm, x2d, w_out_b, pool_width, batch, seq_len)
    return x2d.reshape(batch, seq_len, d)
```

```python
import functools

import jax
import jax.numpy as jnp
from jax import lax
from jax.experimental import pallas as pl
from jax.experimental.pallas import tpu as pltpu

F32 = jnp.float32
BF16 = jnp.bfloat16

EPS = 1e-6
HEAD_DIM = 64
POOL_WINDOWS = (2, 4, 8, 16)
MAX_WINDOW = max(POOL_WINDOWS)
MEM_HEADS = 4
LANES = 128
HEADS_PER_LANE_TILE = LANES // HEAD_DIM
PV_GROUP = HEADS_PER_LANE_TILE
PV_WIDTH = PV_GROUP * HEAD_DIM

TM = 512
TM_SUB = 256
TQ = 256
Q_TILES = 4
V7X_VMEM_BYTES = 64 * 1024 * 1024
TEMPORARIES_BYTES = 8 * 1024 * 1024


def _nbytes(shape, dtype):
    n = jnp.dtype(dtype).itemsize
    for s in shape:
        n *= s
    return n


def _vmem_limit(streamed, resident, scratch):
    total = (2 * sum(_nbytes(*b) for b in streamed) + sum(_nbytes(*b) for b in resident)
             + sum(_nbytes(*b) for b in scratch) + TEMPORARIES_BYTES)
    assert total <= V7X_VMEM_BYTES, total
    return total


def _silu(g):
    return g / (1.0 + jnp.exp(-g))


def _dot(a, b):
    return jnp.dot(a, b, preferred_element_type=F32)


def _dot_nt(a, b):
    return lax.dot_general(a, b, (((1,), (1,)), ((), ())), preferred_element_type=F32)


def _per_head(g):
    return jnp.concatenate([g] * MEM_HEADS, axis=1)


def _block_diag(w_ref):
    g, c, _ = w_ref.shape
    zero = jnp.zeros((c, c), F32)
    rows = [jnp.concatenate([w_ref[i] if j == i else zero for j in range(g)], axis=1) for i in range(g)]
    return jnp.concatenate(rows, axis=0).astype(BF16)


def _segment_mean_matrix(width, seg):
    r = lax.broadcasted_iota(jnp.int32, (width, width), 0) // seg
    c = lax.broadcasted_iota(jnp.int32, (width, width), 1) // seg
    return jnp.where(r == c, 1.0 / seg, 0.0).astype(BF16)


def _mem_kv_kernel(mem_ref, g_ref, w_ref, kg_ref, w_in_ref, w_out_ref,
                   k_out, v_out, w_in_out, w_out_out):
    w_in_out[...] = w_in_ref[...].astype(BF16)
    w_out_out[...] = w_out_ref[...].astype(BF16)

    m = mem_ref[...]
    ms = jnp.mean(m * m, axis=-1, keepdims=True)
    h = (m * lax.rsqrt(ms + EPS) * g_ref[...]).astype(BF16)
    kv = _dot(h, w_ref[...].astype(BF16))
    width = kv.shape[1] // 2
    k = kv[:, :width]
    v = kv[:, width:]
    seg = _segment_mean_matrix(width, width // MEM_HEADS)
    kms = _dot((k * k).astype(BF16), seg)
    k_out[...] = (k * lax.rsqrt(kms + EPS) * _per_head(kg_ref[...])).astype(BF16)
    v_out[...] = v.astype(BF16)


def _mem_kv(mem2d, mem_norm_g, w_mem_kv, k_norm_g, w_in, w_out):
    rows, d = mem2d.shape
    width = w_mem_kv.shape[1] // 2
    steps = 4
    assert rows % steps == 0 and w_in.shape[0] % steps == 0 and w_out.shape[0] % steps == 0
    chunk = lambda a: pl.BlockSpec((a.shape[0] // steps, a.shape[1]), lambda i: (i, 0))
    full = lambda a: pl.BlockSpec(a.shape, lambda i: (0, 0))
    kv_sd = jax.ShapeDtypeStruct((rows, width), BF16)
    kv_spec = pl.BlockSpec((rows // steps, width), lambda i: (i, 0))
    chunk_shape = lambda a: (a.shape[0] // steps, a.shape[1])
    vmem = _vmem_limit(
        streamed=[(chunk_shape(mem2d), F32), (chunk_shape(w_in), F32), (chunk_shape(w_out), F32),
                  (chunk_shape(w_in), BF16), (chunk_shape(w_out), BF16)] + 2 * [(kv_spec.block_shape, BF16)],
        resident=[(w_mem_kv.shape, F32), (mem_norm_g.shape, F32), (k_norm_g.shape, F32)],
        scratch=[])
    return pl.pallas_call(
        _mem_kv_kernel,
        out_shape=(kv_sd, kv_sd, jax.ShapeDtypeStruct(w_in.shape, BF16),
                   jax.ShapeDtypeStruct(w_out.shape, BF16)),
        grid=(steps,),
        in_specs=[chunk(mem2d), full(mem_norm_g), full(w_mem_kv), full(k_norm_g),
                  chunk(w_in), chunk(w_out)],
        out_specs=(kv_spec, kv_spec, chunk(w_in), chunk(w_out)),
        compiler_params=pltpu.CompilerParams(dimension_semantics=("arbitrary",),
                                             vmem_limit_bytes=vmem),
        name="mem_kv",
    )(mem2d, mem_norm_g, w_mem_kv, k_norm_g, w_in, w_out)


DONE, MIX_READY = object(), object()

def _in_proj_kernel(seq_len, pool_w, sb_w, mem_w,
                    x_ref, g_ref, w_ref, pw_ref, pscale_ref, mk_ref, mv_ref, qg_ref,
                    q_out, k_out, v_out, gate_out, pm_out, pbuf, halo_ref):
    tm = x_ref.shape[0]
    n_sub = pbuf.shape[0]
    rows = tm // n_sub
    t0 = (pl.program_id(0) * tm) % seq_len
    head_w = mem_w // MEM_HEADS

    @pl.when(t0 == 0)
    def _():
        halo_ref[...] = jnp.zeros(halo_ref.shape, F32)

    lane = lax.broadcasted_iota(jnp.int32, (rows, pool_w), 1)
    group = lane // (pool_w // len(POOL_WINDOWS))
    row_iota = lax.broadcasted_iota(jnp.int32, (rows, pool_w), 0)
    seg = _segment_mean_matrix(mem_w, head_w)
    q_gain = _per_head(qg_ref[...])
    pool_mix = _block_diag(pw_ref)
    lane_q = lax.broadcasted_iota(jnp.int32, (rows, mem_w), 1) // head_w
    mv = mv_ref[...].astype(F32)
    lane_v = lax.broadcasted_iota(jnp.int32, mv.shape, 1) // head_w
    mv_heads = [jnp.where(lane_v == hd, mv, 0.0).astype(BF16) for hd in range(MEM_HEADS)]

    h, kept = {}, {}

    def normalize(s, n_chunks=4):
        chunk = rows // n_chunks
        parts = []
        for i in range(n_chunks):
            x = x_ref[s * rows + i * chunk:s * rows + (i + 1) * chunk, :]
            ms = jnp.mean(x * x, axis=-1, keepdims=True)
            parts.append((x * lax.rsqrt(ms + EPS) * g_ref[...]).astype(BF16))
            if i + 1 < n_chunks:
                yield
        h[s] = jnp.concatenate(parts, axis=0)

    def project(s):
        hs = h.pop(s)
        rs = slice(s * rows, (s + 1) * rows)

        def proj(c0, width):
            return _dot(hs, w_ref[:, c0:c0 + width])

        c_pool, c_sb, c_mem = 0, 2 * pool_w, 2 * pool_w + 4 * sb_w
        pool_v = proj(c_pool, pool_w)
        yield
        pool_g = proj(c_pool + pool_w, pool_w)
        yield
        mem_q = proj(c_mem, mem_w)
        yield
        mem_g = proj(c_mem + mem_w, mem_w)
        kept[s] = (pool_v, pool_g, mem_q, mem_g)
        yield MIX_READY
        q_out[rs, :] = (proj(c_sb, sb_w) * HEAD_DIM ** -0.5).astype(BF16)
        yield
        k_out[rs, :] = proj(c_sb + sb_w, sb_w).astype(BF16)
        yield
        v_out[rs, :] = proj(c_sb + 2 * sb_w, sb_w).astype(BF16)
        yield
        gate_out[rs, :] = _silu(proj(c_sb + 3 * sb_w, sb_w)).astype(BF16)

    def mix(s):
        pool_v, pool_g, mem_q, mem_g = kept.pop(s)
        rs = slice(s * rows, (s + 1) * rows)

        pbuf[s, 0:MAX_WINDOW, :] = halo_ref[...]
        pbuf[s, MAX_WINDOW:MAX_WINDOW + rows, :] = pool_v
        halo_ref[...] = pool_v[rows - MAX_WINDOW:, :]
        t_seq = t0 + s * rows + row_iota
        win_sum = pool_v
        window = jnp.full((rows, pool_w), 1, jnp.int32)
        acc = pool_v
        shift = 1
        for g, w in enumerate(POOL_WINDOWS):
            while shift < w:
                acc = acc + pbuf[s, MAX_WINDOW - shift:MAX_WINDOW - shift + rows, :]
                shift += 1
            win_sum = jnp.where(group == g, acc, win_sum)
            window = jnp.where(group == g, w, window)
        count = jnp.minimum(t_seq + 1, window).astype(F32)
        pooled = (win_sum / count - pool_v).astype(BF16)
        y_pool = _dot(pooled, pool_mix)
        yield

        qms = _dot((mem_q * mem_q).astype(BF16), seg)
        yield
        pm_out[rs, 0:pool_w] = (y_pool * pscale_ref[...] * _silu(pool_g)).astype(BF16)
        qn = mem_q * lax.rsqrt(qms + EPS) * q_gain
        qn = qn * (head_w ** -0.5)

        scores = []
        for hd in range(MEM_HEADS):
            scores.append(_dot_nt(jnp.where(lane_q == hd, qn, 0.0).astype(BF16), mk_ref[...]))
            yield
        y_mem = None
        for hd in range(MEM_HEADS):
            e = jnp.exp(scores[hd] - jnp.max(scores[hd], axis=-1, keepdims=True))
            pv = _dot(e.astype(BF16), mv_heads[hd])
            yield
            pv = pv * (1.0 / jnp.sum(e, axis=-1, keepdims=True))
            y_mem = pv if y_mem is None else y_mem + pv
        pm_out[rs, pool_w:pool_w + mem_w] = (y_mem * _silu(mem_g)).astype(BF16)

    def sub_tile(s):
        yield from normalize(s)
        yield from project(s)

    stages = [(0, sub_tile(0))]
    while stages:
        for entry in list(stages):
            s, stage = entry
            event = next(stage, DONE)
            if event is DONE:
                stages.remove(entry)
            elif event is MIX_READY:
                stages.append((s, mix(s)))
                if s + 1 < n_sub:
                    stages.append((s + 1, sub_tile(s + 1)))


def _in_proj(x2d, norm_g, w_in, pool_w3, pool_scale, mem_k, mem_v, q_norm_g, seq_len, mem_len):
    rows, d = x2d.shape
    pool_w = pool_w3.shape[0] * pool_w3.shape[1]
    mem_w = mem_k.shape[1]
    sb_w = (w_in.shape[1] - 2 * pool_w - 2 * mem_w) // 4
    assert seq_len % TM == 0 and rows % TM == 0
    tiles_per_seq = seq_len // TM
    row_spec = lambda width: pl.BlockSpec((TM, width), lambda i: (i, 0))
    full = lambda a: pl.BlockSpec(a.shape, lambda i: (0,) * a.ndim)
    mem_spec = pl.BlockSpec((mem_len, mem_w), lambda i: (i // tiles_per_seq, 0))
    out_sd = lambda width: jax.ShapeDtypeStruct((rows, width), BF16)
    scratch = [((TM // TM_SUB, MAX_WINDOW + TM_SUB, pool_w), F32), ((MAX_WINDOW, pool_w), F32)]
    vmem = _vmem_limit(
        streamed=[((TM, d), F32)] + 4 * [((TM, sb_w), BF16)] + [((TM, pool_w + mem_w), BF16)]
        + 2 * [((mem_len, mem_w), BF16)],
        resident=[(w_in.shape, BF16), (pool_w3.shape, F32), (norm_g.shape, F32),
                  (pool_scale.shape, F32), (q_norm_g.shape, F32)],
        scratch=scratch)
    return pl.pallas_call(
        functools.partial(_in_proj_kernel, seq_len, pool_w, sb_w, mem_w),
        out_shape=(out_sd(sb_w), out_sd(sb_w), out_sd(sb_w), out_sd(sb_w), out_sd(pool_w + mem_w)),
        grid=(rows // TM,),
        in_specs=[row_spec(d), full(norm_g), full(w_in), full(pool_w3), full(pool_scale),
                  mem_spec, mem_spec, full(q_norm_g)],
        out_specs=(row_spec(sb_w), row_spec(sb_w), row_spec(sb_w), row_spec(sb_w),
                   row_spec(pool_w + mem_w)),
        scratch_shapes=[pltpu.VMEM(*s) for s in scratch],
        compiler_params=pltpu.CompilerParams(dimension_semantics=("arbitrary",),
                                             vmem_limit_bytes=vmem),
        name="in_proj",
    )(x2d, norm_g, w_in, pool_w3, pool_scale, mem_k, mem_v, q_norm_g)


def _sb_out_kernel(n_heads, pool_w,
                   q_ref, k_ref, v_ref, gate_ref, pm_ref, x_ref, w_ref,
                   o_ref, tri_ref, qm_ref, vm_ref, acc_ref, carry_ref, mixed_ref):
    tq = tri_ref.shape[0]
    n_tiles = q_ref.shape[0] // tq
    n_pairs = n_heads // HEADS_PER_LANE_TILE
    first_block = pl.program_id(1) * n_tiles
    half = lax.broadcasted_iota(jnp.int32, (1, LANES), 1) // HEAD_DIM

    @pl.when(pl.program_id(1) == 0)
    def _():
        lane_head = (lax.broadcasted_iota(jnp.int32, (tq, v_ref.shape[1]), 1) // HEAD_DIM) % PV_GROUP
        for kb in range(v_ref.shape[0] // tq):
            v = v_ref[kb * tq:(kb + 1) * tq, :].astype(F32)
            for g in range(PV_GROUP):
                vm_ref[kb, g * tq:(g + 1) * tq, :] = jnp.where(lane_head == g, v, 0.0).astype(BF16)

    for t in range(n_tiles):
        for p in range(n_pairs):
            qp = q_ref[t * tq:(t + 1) * tq, p * LANES:(p + 1) * LANES].astype(F32)
            for hh in range(HEADS_PER_LANE_TILE):
                qm_ref[t, p * HEADS_PER_LANE_TILE + hh] = jnp.where(half == hh, qp, 0.0).astype(BF16)

    row = lax.broadcasted_iota(jnp.int32, (tq, tq), 0)
    col = lax.broadcasted_iota(jnp.int32, (tq, tq), 1)
    causal = col < row
    tri_ref[...] = jnp.where(row >= col, -1.0, 0.0).astype(BF16)

    def emit(units):
        logit, suffix, a = {}, {}, {}

        def logits(u):
            t, hd, kb, _ = units[u]
            p = hd // HEADS_PER_LANE_TILE
            k0 = pl.multiple_of(kb * tq, tq)
            logit[u] = _dot_nt(qm_ref[t, hd], k_ref[pl.ds(k0, tq), p * LANES:(p + 1) * LANES])

        def suffix_sum(u):
            diagonal = units[u][3]
            z = logit[u].astype(BF16)
            sp = jnp.maximum(z, 0.0) + jnp.log(1.0 + jnp.exp(-jnp.abs(z)))
            if diagonal:
                sp = jnp.where(causal, sp, 0.0)
            suffix[u] = _dot(sp, tri_ref[...])

        def weights(u):
            t, hd, _, diagonal = units[u]
            log_a = logit.pop(u) + suffix[u]
            if not diagonal:
                log_a = log_a + carry_ref[t, hd]
            w = jnp.exp(log_a)
            if diagonal:
                w = jnp.where(causal, w, 0.0)
                carry_ref[t, hd] = suffix.pop(u)[:, 0:1]
            else:
                carry_ref[t, hd] = carry_ref[t, hd] + suffix.pop(u)[:, 0:1]
            a[u] = w.astype(BF16)

        def weighted_values(u_last):
            t, hd, kb, diagonal = units[u_last]
            grp = hd // PV_GROUP
            a_cat = jnp.concatenate([a.pop(u) for u in range(u_last - PV_GROUP + 1, u_last + 1)], axis=1)
            out = _dot(a_cat, vm_ref[kb, :, grp * PV_WIDTH:(grp + 1) * PV_WIDTH])
            if diagonal:
                acc_ref[t, grp] = out
            else:
                acc_ref[t, grp] = acc_ref[t, grp] + out

        n = len(units)
        for step in range(n + 2):
            if step < n:
                logits(step)
            if 0 <= step - 1 < n:
                suffix_sum(step - 1)
            if 0 <= step - 2 < n:
                weights(step - 2)
                if units[step - 2][1] % PV_GROUP == PV_GROUP - 1:
                    weighted_values(step - 2)

    def head_units(t, kb, diagonal):
        return [(t, hd, kb, diagonal) for hd in range(n_heads)]

    units = []
    for depth in range(n_tiles):
        for t in range(depth, n_tiles):
            units += head_units(t, first_block + t - depth, depth == 0)
    emit(units)

    def body(j, _):
        kb = first_block - 1 - j
        units = []
        for grp in range(n_heads // PV_GROUP):
            for t in range(n_tiles):
                units += [(t, grp * PV_GROUP + g, kb, False) for g in range(PV_GROUP)]
        emit(units)
        return 0

    lax.fori_loop(0, first_block, body, 0)

    for t in range(n_tiles):
        rs = slice(t * tq, (t + 1) * tq)
        for grp in range(n_heads // PV_GROUP):
            sl = slice(grp * PV_WIDTH, (grp + 1) * PV_WIDTH)
            mixed_ref[rs, sl] = (acc_ref[t, grp] * gate_ref[rs, sl].astype(F32)).astype(BF16)
    sb_w = mixed_ref.shape[1]
    o_ref[...] = (x_ref[...] + _dot(mixed_ref[...], w_ref[pool_w:pool_w + sb_w, :])
                  + _dot(pm_ref[:, :pool_w], w_ref[:pool_w, :])
                  + _dot(pm_ref[:, pool_w:], w_ref[pool_w + sb_w:, :]))


def _sb_out(q, k, v, gate, pm, x2d, w_out, pool_w, batch, seq_len):
    rows, sb_w = q.shape
    d = x2d.shape[1]
    n_heads = sb_w // HEAD_DIM
    step_rows = TQ * Q_TILES
    assert seq_len % step_rows == 0
    steps = seq_len // step_rows
    k3 = k.reshape(batch, seq_len, sb_w)
    v3 = v.reshape(batch, seq_len, sb_w)
    row_spec = lambda width: pl.BlockSpec((step_rows, width), lambda b, i: (b * steps + i, 0))
    seq_spec = pl.BlockSpec((None, seq_len, sb_w), lambda b, i: (b, 0, 0))
    full = lambda a: pl.BlockSpec(a.shape, lambda b, i: (0, 0))
    scratch = [((TQ, TQ), BF16),
               ((Q_TILES, n_heads, TQ, LANES), BF16),
               ((seq_len // TQ, PV_GROUP * TQ, sb_w), BF16),
               ((Q_TILES, n_heads // PV_GROUP, TQ, PV_WIDTH), F32),
               ((Q_TILES, n_heads, TQ, 1), F32),
               ((step_rows, sb_w), BF16)]
    carry_in_vmem = ((Q_TILES, n_heads, TQ, LANES), F32)
    vmem = _vmem_limit(
        streamed=2 * [((step_rows, sb_w), BF16)] + [((step_rows, pm.shape[1]), BF16)]
        + 2 * [((seq_len, sb_w), BF16)] + 2 * [((step_rows, d), F32)],
        resident=[(w_out.shape, BF16)],
        scratch=scratch + [carry_in_vmem])
    return pl.pallas_call(
        functools.partial(_sb_out_kernel, n_heads, pool_w),
        out_shape=jax.ShapeDtypeStruct((rows, d), F32),
        grid=(batch, steps),
        in_specs=[row_spec(sb_w), seq_spec, seq_spec, row_spec(sb_w), row_spec(pm.shape[1]),
                  row_spec(d), full(w_out)],
        out_specs=row_spec(d),
        scratch_shapes=[pltpu.VMEM(*s) for s in scratch],
        compiler_params=pltpu.CompilerParams(dimension_semantics=("arbitrary", "arbitrary"),
                                             vmem_limit_bytes=vmem),
        name="sb_out",
    )(q, k3, v3, gate, pm, x2d, w_out)


def kernel(x, mem, norm_g, w_in, pool_w, pool_scale, mem_norm_g, w_mem_kv, q_norm_g, k_norm_g, w_out):
    batch, seq_len, d = x.shape
    mem_len = mem.shape[1]
    depth = norm_g.shape[0]
    pool_width = pool_scale.shape[1]
    assert seq_len % TQ == 0
    x2d = x.reshape(batch * seq_len, d)
    for l in range(depth):
        mem_k, mem_v, w_in_b, w_out_b = _mem_kv(mem.reshape(batch * mem_len, d), mem_norm_g[l][None],
                                                w_mem_kv[l], k_norm_g[l][None], w_in[l], w_out[l])
        q, k, v, gate, pm = _in_proj(x2d, norm_g[l][None], w_in_b, pool_w[l],
                                     pool_scale[l][None], mem_k, mem_v, q_norm_g[l][None],
                                     seq_len, mem_len)
        x2d = _sb_out(q, k, v, gate, pm, x2d, w_out_b, pool_width, batch, seq_len)
    return x2d.reshape(batch, seq_len, d)
```

```python
import functools

import jax
import jax.numpy as jnp
from jax import lax
from jax.experimental import pallas as pl
from jax.experimental.pallas import tpu as pltpu

F32 = jnp.float32
BF16 = jnp.bfloat16

EPS = 1e-6
HEAD_DIM = 64
POOL_WINDOWS = (2, 4, 8, 16)
MAX_WINDOW = max(POOL_WINDOWS)
MEM_HEADS = 4
LANES = 128
HEADS_PER_LANE_TILE = LANES // HEAD_DIM
PV_GROUP = HEADS_PER_LANE_TILE
PV_WIDTH = PV_GROUP * HEAD_DIM

MEM_KV_STEPS = 4
TM = 512
TM_SUB = 256
NORM_CHUNKS = 4
TQ = 256
Q_TILES = 4
V7X_VMEM_BYTES = 64 * 1024 * 1024
TEMPORARIES_BYTES = 8 * 1024 * 1024


def _nbytes(shape, dtype):
    n = jnp.dtype(dtype).itemsize
    for s in shape:
        n *= s
    return n


def _vmem_limit(streamed, resident, scratch):
    total = (2 * sum(_nbytes(*b) for b in streamed) + sum(_nbytes(*b) for b in resident)
             + sum(_nbytes(*b) for b in scratch) + TEMPORARIES_BYTES)
    assert total <= V7X_VMEM_BYTES, total
    return total


def _silu(g):
    return g / (1.0 + jnp.exp(-g))


def _dot(a, b):
    return jnp.dot(a, b, preferred_element_type=F32)


def _dot_nt(a, b):
    return lax.dot_general(a, b, (((1,), (1,)), ((), ())), preferred_element_type=F32)


def _per_head(g):
    return jnp.concatenate([g] * MEM_HEADS, axis=1)


def _block_diag(w_ref):
    g, c, _ = w_ref.shape
    zero = jnp.zeros((c, c), F32)
    rows = [jnp.concatenate([w_ref[i] if j == i else zero for j in range(g)], axis=1) for i in range(g)]
    return jnp.concatenate(rows, axis=0).astype(BF16)


def _segment_mean_matrix(width, seg):
    r = lax.broadcasted_iota(jnp.int32, (width, width), 0) // seg
    c = lax.broadcasted_iota(jnp.int32, (width, width), 1) // seg
    return jnp.where(r == c, 1.0 / seg, 0.0).astype(BF16)


def _mem_kv_kernel(mem_ref, g_ref, w_ref, kg_ref, w_in_ref, w_out_ref,
                   k_out, v_out, w_in_out, w_out_out):
    w_in_out[...] = w_in_ref[...].astype(BF16)
    w_out_out[...] = w_out_ref[...].astype(BF16)

    m = mem_ref[...]
    ms = jnp.mean(m * m, axis=-1, keepdims=True)
    h = (m * lax.rsqrt(ms + EPS) * g_ref[...]).astype(BF16)
    kv = _dot(h, w_ref[...].astype(BF16))
    width = kv.shape[1] // 2
    k = kv[:, :width]
    v = kv[:, width:]
    seg = _segment_mean_matrix(width, width // MEM_HEADS)
    kms = _dot((k * k).astype(BF16), seg)
    k_out[...] = (k * lax.rsqrt(kms + EPS) * _per_head(kg_ref[...])).astype(BF16)
    v_out[...] = v.astype(BF16)


def _mem_kv(mem2d, mem_norm_g, w_mem_kv, k_norm_g, w_in, w_out):
    rows, d = mem2d.shape
    width = w_mem_kv.shape[1] // 2
    steps = MEM_KV_STEPS
    assert rows % steps == 0 and w_in.shape[0] % steps == 0 and w_out.shape[0] % steps == 0
    chunk = lambda a: pl.BlockSpec((a.shape[0] // steps, a.shape[1]), lambda i: (i, 0))
    full = lambda a: pl.BlockSpec(a.shape, lambda i: (0, 0))
    kv_sd = jax.ShapeDtypeStruct((rows, width), BF16)
    kv_spec = pl.BlockSpec((rows // steps, width), lambda i: (i, 0))
    chunk_shape = lambda a: (a.shape[0] // steps, a.shape[1])
    vmem = _vmem_limit(
        streamed=[(chunk_shape(mem2d), F32), (chunk_shape(w_in), F32), (chunk_shape(w_out), F32),
                  (chunk_shape(w_in), BF16), (chunk_shape(w_out), BF16)] + 2 * [(kv_spec.block_shape, BF16)],
        resident=[(w_mem_kv.shape, F32), (mem_norm_g.shape, F32), (k_norm_g.shape, F32)],
        scratch=[])
    return pl.pallas_call(
        _mem_kv_kernel,
        out_shape=(kv_sd, kv_sd, jax.ShapeDtypeStruct(w_in.shape, BF16),
                   jax.ShapeDtypeStruct(w_out.shape, BF16)),
        grid=(steps,),
        in_specs=[chunk(mem2d), full(mem_norm_g), full(w_mem_kv), full(k_norm_g),
                  chunk(w_in), chunk(w_out)],
        out_specs=(kv_spec, kv_spec, chunk(w_in), chunk(w_out)),
        compiler_params=pltpu.CompilerParams(dimension_semantics=("arbitrary",),
                                             vmem_limit_bytes=vmem),
        name="mem_kv",
    )(mem2d, mem_norm_g, w_mem_kv, k_norm_g, w_in, w_out)


DONE, MIX_READY = object(), object()


def _in_proj_kernel(seq_len, pool_w, sb_w, mem_w,
                    x_ref, g_ref, w_ref, pw_ref, pscale_ref, mk_ref, mv_ref, qg_ref,
                    q_out, k_out, v_out, gate_out, pm_out, pbuf, halo_ref):
    tm = x_ref.shape[0]
    n_sub = pbuf.shape[0]
    rows = tm // n_sub
    t0 = (pl.program_id(0) * tm) % seq_len
    head_w = mem_w // MEM_HEADS

    @pl.when(t0 == 0)
    def _():
        halo_ref[...] = jnp.zeros(halo_ref.shape, F32)

    lane = lax.broadcasted_iota(jnp.int32, (rows, pool_w), 1)
    group = lane // (pool_w // len(POOL_WINDOWS))
    row_iota = lax.broadcasted_iota(jnp.int32, (rows, pool_w), 0)
    seg = _segment_mean_matrix(mem_w, head_w)
    q_gain = _per_head(qg_ref[...])
    pool_mix = _block_diag(pw_ref)
    lane_q = lax.broadcasted_iota(jnp.int32, (rows, mem_w), 1) // head_w
    mv = mv_ref[...].astype(F32)
    lane_v = lax.broadcasted_iota(jnp.int32, mv.shape, 1) // head_w
    mv_heads = [jnp.where(lane_v == hd, mv, 0.0).astype(BF16) for hd in range(MEM_HEADS)]

    h, kept = {}, {}

    def normalize(s):
        chunk = rows // NORM_CHUNKS
        parts = []
        for i in range(NORM_CHUNKS):
            x = x_ref[s * rows + i * chunk:s * rows + (i + 1) * chunk, :]
            ms = jnp.mean(x * x, axis=-1, keepdims=True)
            parts.append((x * lax.rsqrt(ms + EPS) * g_ref[...]).astype(BF16))
            if i + 1 < NORM_CHUNKS:
                yield
        h[s] = jnp.concatenate(parts, axis=0)

    def project(s):
        hs = h.pop(s)
        rs = slice(s * rows, (s + 1) * rows)

        def proj(c0, width):
            return _dot(hs, w_ref[:, c0:c0 + width])

        c_pool, c_sb, c_mem = 0, 2 * pool_w, 2 * pool_w + 4 * sb_w
        pool_v = proj(c_pool, pool_w)
        yield
        pool_g = proj(c_pool + pool_w, pool_w)
        yield
        mem_q = proj(c_mem, mem_w)
        yield
        mem_g = proj(c_mem + mem_w, mem_w)
        kept[s] = (pool_v, pool_g, mem_q, mem_g)
        yield MIX_READY
        q_out[rs, :] = (proj(c_sb, sb_w) * HEAD_DIM ** -0.5).astype(BF16)
        yield
        k_out[rs, :] = proj(c_sb + sb_w, sb_w).astype(BF16)
        yield
        v_out[rs, :] = proj(c_sb + 2 * sb_w, sb_w).astype(BF16)
        yield
        gate_out[rs, :] = _silu(proj(c_sb + 3 * sb_w, sb_w)).astype(BF16)

    def mix(s):
        pool_v, pool_g, mem_q, mem_g = kept.pop(s)
        rs = slice(s * rows, (s + 1) * rows)

        pbuf[s, 0:MAX_WINDOW, :] = halo_ref[...]
        pbuf[s, MAX_WINDOW:MAX_WINDOW + rows, :] = pool_v
        halo_ref[...] = pool_v[rows - MAX_WINDOW:, :]
        t_seq = t0 + s * rows + row_iota
        win_sum = pool_v
        window = jnp.full((rows, pool_w), 1, jnp.int32)
        acc = pool_v
        shift = 1
        for g, w in enumerate(POOL_WINDOWS):
            while shift < w:
                acc = acc + pbuf[s, MAX_WINDOW - shift:MAX_WINDOW - shift + rows, :]
                shift += 1
            win_sum = jnp.where(group == g, acc, win_sum)
            window = jnp.where(group == g, w, window)
        count = jnp.minimum(t_seq + 1, window).astype(F32)
        pooled = (win_sum / count - pool_v).astype(BF16)
        y_pool = _dot(pooled, pool_mix)
        yield

        qms = _dot((mem_q * mem_q).astype(BF16), seg)
        yield
        pm_out[rs, 0:pool_w] = (y_pool * pscale_ref[...] * _silu(pool_g)).astype(BF16)
        qn = mem_q * lax.rsqrt(qms + EPS) * q_gain
        qn = qn * (head_w ** -0.5)

        scores = []
        for hd in range(MEM_HEADS):
            scores.append(_dot_nt(jnp.where(lane_q == hd, qn, 0.0).astype(BF16), mk_ref[...]))
            yield
        y_mem = None
        for hd in range(MEM_HEADS):
            e = jnp.exp(scores[hd] - jnp.max(scores[hd], axis=-1, keepdims=True))
            pv = _dot(e.astype(BF16), mv_heads[hd])
            yield
            pv = pv * (1.0 / jnp.sum(e, axis=-1, keepdims=True))
            y_mem = pv if y_mem is None else y_mem + pv
        pm_out[rs, pool_w:pool_w + mem_w] = (y_mem * _silu(mem_g)).astype(BF16)

    def sub_tile(s):
        yield from normalize(s)
        yield from project(s)

    stages = [(0, sub_tile(0))]
    while stages:
        for entry in list(stages):
            s, stage = entry
            event = next(stage, DONE)
            if event is DONE:
                stages.remove(entry)
            elif event is MIX_READY:
                stages.append((s, mix(s)))
                if s + 1 < n_sub:
                    stages.append((s + 1, sub_tile(s + 1)))


def _in_proj(x2d, norm_g, w_in, pool_w3, pool_scale, mem_k, mem_v, q_norm_g, seq_len, mem_len):
    rows, d = x2d.shape
    pool_w = pool_w3.shape[0] * pool_w3.shape[1]
    mem_w = mem_k.shape[1]
    sb_w = (w_in.shape[1] - 2 * pool_w - 2 * mem_w) // 4
    assert seq_len % TM == 0 and rows % TM == 0
    tiles_per_seq = seq_len // TM
    row_spec = lambda width: pl.BlockSpec((TM, width), lambda i: (i, 0))
    full = lambda a: pl.BlockSpec(a.shape, lambda i: (0,) * a.ndim)
    mem_spec = pl.BlockSpec((mem_len, mem_w), lambda i: (i // tiles_per_seq, 0))
    out_sd = lambda width: jax.ShapeDtypeStruct((rows, width), BF16)
    scratch = [((TM // TM_SUB, MAX_WINDOW + TM_SUB, pool_w), F32), ((MAX_WINDOW, pool_w), F32)]
    vmem = _vmem_limit(
        streamed=[((TM, d), F32)] + 4 * [((TM, sb_w), BF16)] + [((TM, pool_w + mem_w), BF16)]
        + 2 * [((mem_len, mem_w), BF16)],
        resident=[(w_in.shape, BF16), (pool_w3.shape, F32), (norm_g.shape, F32),
                  (pool_scale.shape, F32), (q_norm_g.shape, F32)],
        scratch=scratch)
    return pl.pallas_call(
        functools.partial(_in_proj_kernel, seq_len, pool_w, sb_w, mem_w),
        out_shape=(out_sd(sb_w), out_sd(sb_w), out_sd(sb_w), out_sd(sb_w), out_sd(pool_w + mem_w)),
        grid=(rows // TM,),
        in_specs=[row_spec(d), full(norm_g), full(w_in), full(pool_w3), full(pool_scale),
                  mem_spec, mem_spec, full(q_norm_g)],
        out_specs=(row_spec(sb_w), row_spec(sb_w), row_spec(sb_w), row_spec(sb_w),
                   row_spec(pool_w + mem_w)),
        scratch_shapes=[pltpu.VMEM(*s) for s in scratch],
        compiler_params=pltpu.CompilerParams(dimension_semantics=("arbitrary",),
                                             vmem_limit_bytes=vmem),
        name="in_proj",
    )(x2d, norm_g, w_in, pool_w3, pool_scale, mem_k, mem_v, q_norm_g)


def _sb_out_kernel(n_heads, pool_w,
                   q_ref, k_ref, v_ref, gate_ref, pm_ref, x_ref, w_ref,
                   o_ref, tri_ref, qm_ref, vm_ref, acc_ref, carry_ref, mixed_ref):
    tq = tri_ref.shape[0]
    n_tiles = q_ref.shape[0] // tq
    n_pairs = n_heads // HEADS_PER_LANE_TILE
    first_block = pl.program_id(1) * n_tiles
    half = lax.broadcasted_iota(jnp.int32, (1, LANES), 1) // HEAD_DIM

    @pl.when(pl.program_id(1) == 0)
    def _():
        lane_head = (lax.broadcasted_iota(jnp.int32, (tq, v_ref.shape[1]), 1) // HEAD_DIM) % PV_GROUP
        for kb in range(v_ref.shape[0] // tq):
            v = v_ref[kb * tq:(kb + 1) * tq, :].astype(F32)
            for g in range(PV_GROUP):
                vm_ref[kb, g * tq:(g + 1) * tq, :] = jnp.where(lane_head == g, v, 0.0).astype(BF16)

    for t in range(n_tiles):
        for p in range(n_pairs):
            qp = q_ref[t * tq:(t + 1) * tq, p * LANES:(p + 1) * LANES].astype(F32)
            for hh in range(HEADS_PER_LANE_TILE):
                qm_ref[t, p * HEADS_PER_LANE_TILE + hh] = jnp.where(half == hh, qp, 0.0).astype(BF16)

    row = lax.broadcasted_iota(jnp.int32, (tq, tq), 0)
    col = lax.broadcasted_iota(jnp.int32, (tq, tq), 1)
    causal = col < row
    tri_ref[...] = jnp.where(row >= col, -1.0, 0.0).astype(BF16)

    def emit(units):
        logit, suffix, a = {}, {}, {}

        def logits(u):
            t, hd, kb, _ = units[u]
            p = hd // HEADS_PER_LANE_TILE
            k0 = pl.multiple_of(kb * tq, tq)
            logit[u] = _dot_nt(qm_ref[t, hd], k_ref[pl.ds(k0, tq), p * LANES:(p + 1) * LANES])

        def suffix_sum(u):
            diagonal = units[u][3]
            z = logit[u].astype(BF16)
            sp = jnp.maximum(z, 0.0) + jnp.log(1.0 + jnp.exp(-jnp.abs(z)))
            if diagonal:
                sp = jnp.where(causal, sp, 0.0)
            suffix[u] = _dot(sp, tri_ref[...])

        def weights(u):
            t, hd, _, diagonal = units[u]
            log_a = logit.pop(u) + suffix[u]
            if not diagonal:
                log_a = log_a + carry_ref[t, hd]
            w = jnp.exp(log_a)
            if diagonal:
                w = jnp.where(causal, w, 0.0)
                carry_ref[t, hd] = suffix.pop(u)[:, 0:1]
            else:
                carry_ref[t, hd] = carry_ref[t, hd] + suffix.pop(u)[:, 0:1]
            a[u] = w.astype(BF16)

        def weighted_values(u_last):
            t, hd, kb, diagonal = units[u_last]
            grp = hd // PV_GROUP
            a_cat = jnp.concatenate([a.pop(u) for u in range(u_last - PV_GROUP + 1, u_last + 1)], axis=1)
            out = _dot(a_cat, vm_ref[kb, :, grp * PV_WIDTH:(grp + 1) * PV_WIDTH])
            if diagonal:
                acc_ref[t, grp] = out
            else:
                acc_ref[t, grp] = acc_ref[t, grp] + out

        n = len(units)
        for step in range(n + 2):
            if step < n:
                logits(step)
            if 0 <= step - 1 < n:
                suffix_sum(step - 1)
            if 0 <= step - 2 < n:
                weights(step - 2)
                if units[step - 2][1] % PV_GROUP == PV_GROUP - 1:
                    weighted_values(step - 2)

    def head_units(t, kb, diagonal):
        return [(t, hd, kb, diagonal) for hd in range(n_heads)]

    units = []
    for depth in range(n_tiles):
        for t in range(depth, n_tiles):
            units += head_units(t, first_block + t - depth, depth == 0)
    emit(units)

    def body(j, _):
        kb = first_block - 1 - j
        units = []
        for grp in range(n_heads // PV_GROUP):
            for t in range(n_tiles):
                units += [(t, grp * PV_GROUP + g, kb, False) for g in range(PV_GROUP)]
        emit(units)
        return 0

    lax.fori_loop(0, first_block, body, 0)

    for t in range(n_tiles):
        rs = slice(t * tq, (t + 1) * tq)
        for grp in range(n_heads // PV_GROUP):
            sl = slice(grp * PV_WIDTH, (grp + 1) * PV_WIDTH)
            mixed_ref[rs, sl] = (acc_ref[t, grp] * gate_ref[rs, sl].astype(F32)).astype(BF16)
    sb_w = mixed_ref.shape[1]
    o_ref[...] = (x_ref[...] + _dot(mixed_ref[...], w_ref[pool_w:pool_w + sb_w, :])
                  + _dot(pm_ref[:, :pool_w], w_ref[:pool_w, :])
                  + _dot(pm_ref[:, pool_w:], w_ref[pool_w + sb_w:, :]))


def _sb_out(q, k, v, gate, pm, x2d, w_out, pool_w, batch, seq_len):
    rows, sb_w = q.shape
    d = x2d.shape[1]
    n_heads = sb_w // HEAD_DIM
    step_rows = TQ * Q_TILES
    assert seq_len % step_rows == 0
    steps = seq_len // step_rows
    k3 = k.reshape(batch, seq_len, sb_w)
    v3 = v.reshape(batch, seq_len, sb_w)
    row_spec = lambda width: pl.BlockSpec((step_rows, width), lambda b, i: (b * steps + i, 0))
    seq_spec = pl.BlockSpec((None, seq_len, sb_w), lambda b, i: (b, 0, 0))
    full = lambda a: pl.BlockSpec(a.shape, lambda b, i: (0, 0))
    scratch = [((TQ, TQ), BF16),
               ((Q_TILES, n_heads, TQ, LANES), BF16),
               ((seq_len // TQ, PV_GROUP * TQ, sb_w), BF16),
               ((Q_TILES, n_heads // PV_GROUP, TQ, PV_WIDTH), F32),
               ((Q_TILES, n_heads, TQ, 1), F32),
               ((step_rows, sb_w), BF16)]
    carry_in_vmem = ((Q_TILES, n_heads, TQ, LANES), F32)
    vmem = _vmem_limit(
        streamed=2 * [((step_rows, sb_w), BF16)] + [((step_rows, pm.shape[1]), BF16)]
        + 2 * [((seq_len, sb_w), BF16)] + 2 * [((step_rows, d), F32)],
        resident=[(w_out.shape, BF16)],
        scratch=scratch + [carry_in_vmem])
    return pl.pallas_call(
        functools.partial(_sb_out_kernel, n_heads, pool_w),
        out_shape=jax.ShapeDtypeStruct((rows, d), F32),
        grid=(batch, steps),
        in_specs=[row_spec(sb_w), seq_spec, seq_spec, row_spec(sb_w), row_spec(pm.shape[1]),
                  row_spec(d), full(w_out)],
        out_specs=row_spec(d),
        scratch_shapes=[pltpu.VMEM(*s) for s in scratch],
        compiler_params=pltpu.CompilerParams(dimension_semantics=("arbitrary", "arbitrary"),
                                             vmem_limit_bytes=vmem),
        name="sb_out",
    )(q, k3, v3, gate, pm, x2d, w_out)


def kernel(x, mem, norm_g, w_in, pool_w, pool_scale, mem_norm_g, w_mem_kv, q_norm_g, k_norm_g, w_out):
    batch, seq_len, d = x.shape
    mem_len = mem.shape[1]
    depth = norm_g.shape[0]
    pool_width = pool_scale.shape[1]
    assert seq_len % TQ == 0
    x2d = x.reshape(batch * seq_len, d)
    for l in range(depth):
        mem_k, mem_v, w_in_b, w_out_b = _mem_kv(mem.reshape(batch * mem_len, d), mem_norm_g[l][None],
                                                w_mem_kv[l], k_norm_g[l][None], w_in[l], w_out[l])
        q, k, v, gate, pm = _in_proj(x2d, norm_g[l][None], w_in_b, pool_w[l],
                                     pool_scale[l][None], mem_k, mem_v, q_norm_g[l][None],
                                     seq_len, mem_len)
        x2d = _sb_out(q, k, v, gate, pm, x2d, w_out_b, pool_width, batch, seq_len)
    return x2d.reshape(batch, seq_len, d)
```

```python
import functools

import jax
import jax.numpy as jnp
from jax import lax
from jax.experimental import pallas as pl
from jax.experimental.pallas import tpu as pltpu

F32 = jnp.float32
BF16 = jnp.bfloat16

EPS = 1e-6
HEAD_DIM = 64
POOL_WINDOWS = (2, 4, 8, 16)
MAX_WINDOW = max(POOL_WINDOWS)
MEM_HEADS = 4
LANES = 128
HEADS_PER_LANE_TILE = LANES // HEAD_DIM
PV_GROUP = HEADS_PER_LANE_TILE
PV_WIDTH = PV_GROUP * HEAD_DIM

MEM_KV_STEPS = 4
TM = 512
TM_SUB = 256
NORM_CHUNKS = 4
TQ = 256
Q_TILES = 4
V7X_VMEM_BYTES = 64 * 1024 * 1024
TEMPORARIES_BYTES = 8 * 1024 * 1024


def _nbytes(shape, dtype):
    n = jnp.dtype(dtype).itemsize
    for s in shape:
        n *= s
    return n


def _vmem_limit(streamed, resident, scratch):
    total = (2 * sum(_nbytes(*b) for b in streamed) + sum(_nbytes(*b) for b in resident)
             + sum(_nbytes(*b) for b in scratch) + TEMPORARIES_BYTES)
    assert total <= V7X_VMEM_BYTES, total
    return total


def _silu(g):
    return g / (1.0 + jnp.exp(-g))


def _dot(a, b):
    return jnp.dot(a, b, preferred_element_type=F32)


def _dot_nt(a, b):
    return lax.dot_general(a, b, (((1,), (1,)), ((), ())), preferred_element_type=F32)


def _per_head(g):
    return jnp.concatenate([g] * MEM_HEADS, axis=1)


def _block_diag(w_ref):
    g, c, _ = w_ref.shape
    zero = jnp.zeros((c, c), F32)
    rows = [jnp.concatenate([w_ref[i] if j == i else zero for j in range(g)], axis=1) for i in range(g)]
    return jnp.concatenate(rows, axis=0).astype(BF16)


def _segment_mean_matrix(width, seg):
    r = lax.broadcasted_iota(jnp.int32, (width, width), 0) // seg
    c = lax.broadcasted_iota(jnp.int32, (width, width), 1) // seg
    return jnp.where(r == c, 1.0 / seg, 0.0).astype(BF16)


def _mem_kv_kernel(mem_ref, g_ref, w_ref, kg_ref, w_in_ref, w_out_ref,
                   k_out, v_out, w_in_out, w_out_out):
    w_in_out[...] = w_in_ref[...].astype(BF16)
    w_out_out[...] = w_out_ref[...].astype(BF16)

    m = mem_ref[...]
    ms = jnp.mean(m * m, axis=-1, keepdims=True)
    h = (m * lax.rsqrt(ms + EPS) * g_ref[...]).astype(BF16)
    kv = _dot(h, w_ref[...].astype(BF16))
    width = kv.shape[1] // 2
    k = kv[:, :width]
    v = kv[:, width:]
    seg = _segment_mean_matrix(width, width // MEM_HEADS)
    kms = _dot((k * k).astype(BF16), seg)
    k_out[...] = (k * lax.rsqrt(kms + EPS) * _per_head(kg_ref[...])).astype(BF16)
    v_out[...] = v.astype(BF16)


def _mem_kv(mem2d, mem_norm_g, w_mem_kv, k_norm_g, w_in, w_out):
    rows, d = mem2d.shape
    width = w_mem_kv.shape[1] // 2
    steps = MEM_KV_STEPS
    assert rows % steps == 0 and w_in.shape[0] % steps == 0 and w_out.shape[0] % steps == 0
    chunk = lambda a: pl.BlockSpec((a.shape[0] // steps, a.shape[1]), lambda i: (i, 0))
    full = lambda a: pl.BlockSpec(a.shape, lambda i: (0, 0))
    kv_sd = jax.ShapeDtypeStruct((rows, width), BF16)
    kv_spec = pl.BlockSpec((rows // steps, width), lambda i: (i, 0))
    chunk_shape = lambda a: (a.shape[0] // steps, a.shape[1])
    vmem = _vmem_limit(
        streamed=[(chunk_shape(mem2d), F32), (chunk_shape(w_in), F32), (chunk_shape(w_out), F32),
                  (chunk_shape(w_in), BF16), (chunk_shape(w_out), BF16)] + 2 * [(kv_spec.block_shape, BF16)],
        resident=[(w_mem_kv.shape, F32), (mem_norm_g.shape, F32), (k_norm_g.shape, F32)],
        scratch=[])
    return pl.pallas_call(
        _mem_kv_kernel,
        out_shape=(kv_sd, kv_sd, jax.ShapeDtypeStruct(w_in.shape, BF16),
                   jax.ShapeDtypeStruct(w_out.shape, BF16)),
        grid=(steps,),
        in_specs=[chunk(mem2d), full(mem_norm_g), full(w_mem_kv), full(k_norm_g),
                  chunk(w_in), chunk(w_out)],
        out_specs=(kv_spec, kv_spec, chunk(w_in), chunk(w_out)),
        compiler_params=pltpu.CompilerParams(dimension_semantics=("arbitrary",),
                                             vmem_limit_bytes=vmem),
        name="mem_kv",
    )(mem2d, mem_norm_g, w_mem_kv, k_norm_g, w_in, w_out)


DONE, MIX_READY = object(), object()


def _in_proj_kernel(seq_len, pool_w, sb_w, mem_w,
                    x_ref, g_ref, w_ref, pw_ref, pscale_ref, mk_ref, mv_ref, qg_ref,
                    q_out, k_out, v_out, gate_out, pm_out, pbuf, halo_ref):
    tm = x_ref.shape[0]
    n_sub = pbuf.shape[0]
    rows = tm // n_sub
    t0 = (pl.program_id(0) * tm) % seq_len
    head_w = mem_w // MEM_HEADS

    @pl.when(t0 == 0)
    def _():
        halo_ref[...] = jnp.zeros(halo_ref.shape, F32)

    lane = lax.broadcasted_iota(jnp.int32, (rows, pool_w), 1)
    group = lane // (pool_w // len(POOL_WINDOWS))
    row_iota = lax.broadcasted_iota(jnp.int32, (rows, pool_w), 0)
    seg = _segment_mean_matrix(mem_w, head_w)
    q_gain = _per_head(qg_ref[...])
    pool_mix = _block_diag(pw_ref)
    lane_q = lax.broadcasted_iota(jnp.int32, (rows, mem_w), 1) // head_w
    mv = mv_ref[...].astype(F32)
    lane_v = lax.broadcasted_iota(jnp.int32, mv.shape, 1) // head_w
    mv_heads = [jnp.where(lane_v == hd, mv, 0.0).astype(BF16) for hd in range(MEM_HEADS)]

    h, kept = {}, {}

    def normalize(s):
        chunk = rows // NORM_CHUNKS
        parts = []
        for i in range(NORM_CHUNKS):
            x = x_ref[s * rows + i * chunk:s * rows + (i + 1) * chunk, :]
            ms = jnp.mean(x * x, axis=-1, keepdims=True)
            parts.append((x * lax.rsqrt(ms + EPS) * g_ref[...]).astype(BF16))
            if i + 1 < NORM_CHUNKS:
                yield
        h[s] = jnp.concatenate(parts, axis=0)

    def project(s):
        hs = h.pop(s)
        rs = slice(s * rows, (s + 1) * rows)

        def proj(c0, width):
            return _dot(hs, w_ref[:, c0:c0 + width])

        c_pool, c_sb, c_mem = 0, 2 * pool_w, 2 * pool_w + 4 * sb_w
        pool_v = proj(c_pool, pool_w)
        yield
        pool_g = proj(c_pool + pool_w, pool_w)
        yield
        mem_q = proj(c_mem, mem_w)
        yield
        mem_g = proj(c_mem + mem_w, mem_w)
        kept[s] = (pool_v, pool_g, mem_q, mem_g)
        yield MIX_READY
        q_out[rs, :] = (proj(c_sb, sb_w) * HEAD_DIM ** -0.5).astype(BF16)
        yield
        k_out[rs, :] = proj(c_sb + sb_w, sb_w).astype(BF16)
        yield
        v_out[rs, :] = proj(c_sb + 2 * sb_w, sb_w).astype(BF16)
        yield
        gate_out[rs, :] = _silu(proj(c_sb + 3 * sb_w, sb_w)).astype(BF16)

    def mix(s):
        pool_v, pool_g, mem_q, mem_g = kept.pop(s)
        rs = slice(s * rows, (s + 1) * rows)

        pbuf[s, 0:MAX_WINDOW, :] = halo_ref[...]
        pbuf[s, MAX_WINDOW:MAX_WINDOW + rows, :] = pool_v
        halo_ref[...] = pool_v[rows - MAX_WINDOW:, :]
        t_seq = t0 + s * rows + row_iota
        win_sum = pool_v
        window = jnp.full((rows, pool_w), 1, jnp.int32)
        acc = pool_v
        shift = 1
        for g, w in enumerate(POOL_WINDOWS):
            while shift < w:
                acc = acc + pbuf[s, MAX_WINDOW - shift:MAX_WINDOW - shift + rows, :]
                shift += 1
            win_sum = jnp.where(group == g, acc, win_sum)
            window = jnp.where(group == g, w, window)
        count = jnp.minimum(t_seq + 1, window).astype(F32)
        pooled = (win_sum / count - pool_v).astype(BF16)
        y_pool = _dot(pooled, pool_mix)
        yield

        qms = _dot((mem_q * mem_q).astype(BF16), seg)
        yield
        pm_out[rs, 0:pool_w] = (y_pool * pscale_ref[...] * _silu(pool_g)).astype(BF16)
        qn = mem_q * lax.rsqrt(qms + EPS) * q_gain
        qn = qn * (head_w ** -0.5)

        scores = []
        for hd in range(MEM_HEADS):
            scores.append(_dot_nt(jnp.where(lane_q == hd, qn, 0.0).astype(BF16), mk_ref[...]))
            yield
        y_mem = None
        for hd in range(MEM_HEADS):
            e = jnp.exp(scores[hd] - jnp.max(scores[hd], axis=-1, keepdims=True))
            pv = _dot(e.astype(BF16), mv_heads[hd])
            yield
            pv = pv * (1.0 / jnp.sum(e, axis=-1, keepdims=True))
            y_mem = pv if y_mem is None else y_mem + pv
        pm_out[rs, pool_w:pool_w + mem_w] = (y_mem * _silu(mem_g)).astype(BF16)

    def sub_tile(s):
        yield from normalize(s)
        yield from project(s)

    stages = [(0, sub_tile(0))]
    while stages:
        for entry in list(stages):
            s, stage = entry
            event = next(stage, DONE)
            if event is DONE:
                stages.remove(entry)
            elif event is MIX_READY:
                stages.append((s, mix(s)))
                if s + 1 < n_sub:
                    stages.append((s + 1, sub_tile(s + 1)))


def _in_proj(x2d, norm_g, w_in, pool_w3, pool_scale, mem_k, mem_v, q_norm_g, seq_len, mem_len):
    rows, d = x2d.shape
    pool_w = pool_w3.shape[0] * pool_w3.shape[1]
    mem_w = mem_k.shape[1]
    sb_w = (w_in.shape[1] - 2 * pool_w - 2 * mem_w) // 4
    assert seq_len % TM == 0 and rows % TM == 0
    tiles_per_seq = seq_len // TM
    row_spec = lambda width: pl.BlockSpec((TM, width), lambda i: (i, 0))
    full = lambda a: pl.BlockSpec(a.shape, lambda i: (0,) * a.ndim)
    mem_spec = pl.BlockSpec((mem_len, mem_w), lambda i: (i // tiles_per_seq, 0))
    out_sd = lambda width: jax.ShapeDtypeStruct((rows, width), BF16)
    scratch = [((TM // TM_SUB, MAX_WINDOW + TM_SUB, pool_w), F32), ((MAX_WINDOW, pool_w), F32)]
    vmem = _vmem_limit(
        streamed=[((TM, d), F32)] + 4 * [((TM, sb_w), BF16)] + [((TM, pool_w + mem_w), BF16)]
        + 2 * [((mem_len, mem_w), BF16)],
        resident=[(w_in.shape, BF16), (pool_w3.shape, F32), (norm_g.shape, F32),
                  (pool_scale.shape, F32), (q_norm_g.shape, F32)],
        scratch=scratch)
    return pl.pallas_call(
        functools.partial(_in_proj_kernel, seq_len, pool_w, sb_w, mem_w),
        out_shape=(out_sd(sb_w), out_sd(sb_w), out_sd(sb_w), out_sd(sb_w), out_sd(pool_w + mem_w)),
        grid=(rows // TM,),
        in_specs=[row_spec(d), full(norm_g), full(w_in), full(pool_w3), full(pool_scale),
                  mem_spec, mem_spec, full(q_norm_g)],
        out_specs=(row_spec(sb_w), row_spec(sb_w), row_spec(sb_w), row_spec(sb_w),
                   row_spec(pool_w + mem_w)),
        scratch_shapes=[pltpu.VMEM(*s) for s in scratch],
        compiler_params=pltpu.CompilerParams(dimension_semantics=("arbitrary",),
                                             vmem_limit_bytes=vmem),
        name="in_proj",
    )(x2d, norm_g, w_in, pool_w3, pool_scale, mem_k, mem_v, q_norm_g)


def _sb_out_kernel(n_heads, pool_w,
                   q_ref, k_ref, v_ref, gate_ref, pm_ref, x_ref, w_ref,
                   o_ref, tri_ref, qm_ref, vm_ref, acc_ref, carry_ref, mixed_ref):
    tq = tri_ref.shape[0]
    n_tiles = q_ref.shape[0] // tq
    n_pairs = n_heads // HEADS_PER_LANE_TILE
    first_block = pl.program_id(1) * n_tiles
    half = lax.broadcasted_iota(jnp.int32, (1, LANES), 1) // HEAD_DIM

    @pl.when(pl.program_id(1) == 0)
    def _():
        lane_head = (lax.broadcasted_iota(jnp.int32, (tq, v_ref.shape[1]), 1) // HEAD_DIM) % PV_GROUP
        for kb in range(v_ref.shape[0] // tq):
            v = v_ref[kb * tq:(kb + 1) * tq, :].astype(F32)
            for g in range(PV_GROUP):
                vm_ref[kb, g * tq:(g + 1) * tq, :] = jnp.where(lane_head == g, v, 0.0).astype(BF16)

    for t in range(n_tiles):
        for p in range(n_pairs):
            qp = q_ref[t * tq:(t + 1) * tq, p * LANES:(p + 1) * LANES].astype(F32)
            for hh in range(HEADS_PER_LANE_TILE):
                qm_ref[t, p * HEADS_PER_LANE_TILE + hh] = jnp.where(half == hh, qp, 0.0).astype(BF16)

    row = lax.broadcasted_iota(jnp.int32, (tq, tq), 0)
    col = lax.broadcasted_iota(jnp.int32, (tq, tq), 1)
    causal = col < row
    tri_ref[...] = jnp.where(row >= col, -1.0, 0.0).astype(BF16)

    def emit(units):
        logit, suffix, a = {}, {}, {}

        def logits(u):
            t, hd, kb, _ = units[u]
            p = hd // HEADS_PER_LANE_TILE
            k0 = pl.multiple_of(kb * tq, tq)
            logit[u] = _dot_nt(qm_ref[t, hd], k_ref[pl.ds(k0, tq), p * LANES:(p + 1) * LANES])

        def suffix_sum(u):
            diagonal = units[u][3]
            z = logit[u].astype(BF16)
            e = jnp.exp(-jnp.abs(logit[u])).astype(BF16)
            sp = jnp.maximum(z, 0.0) + jnp.log(1.0 + e)
            if diagonal:
                sp = jnp.where(causal, sp, 0.0)
            suffix[u] = _dot(sp, tri_ref[...])

        def weights(u):
            t, hd, _, diagonal = units[u]
            log_a = logit.pop(u) + suffix[u]
            if not diagonal:
                log_a = log_a + carry_ref[t, hd]
            w = jnp.exp(log_a)
            if diagonal:
                w = jnp.where(causal, w, 0.0)
                carry_ref[t, hd] = suffix.pop(u)[:, 0:1]
            else:
                carry_ref[t, hd] = carry_ref[t, hd] + suffix.pop(u)[:, 0:1]
            a[u] = w.astype(BF16)

        def weighted_values(u_last):
            t, hd, kb, diagonal = units[u_last]
            grp = hd // PV_GROUP
            a_cat = jnp.concatenate([a.pop(u) for u in range(u_last - PV_GROUP + 1, u_last + 1)], axis=1)
            out = _dot(a_cat, vm_ref[kb, :, grp * PV_WIDTH:(grp + 1) * PV_WIDTH])
            if diagonal:
                acc_ref[t, grp] = out
            else:
                acc_ref[t, grp] = acc_ref[t, grp] + out

        n = len(units)
        for step in range(n + 2):
            if step < n:
                logits(step)
            if 0 <= step - 1 < n:
                suffix_sum(step - 1)
            if 0 <= step - 2 < n:
                weights(step - 2)
                if units[step - 2][1] % PV_GROUP == PV_GROUP - 1:
                    weighted_values(step - 2)

    def head_units(t, kb, diagonal):
        return [(t, hd, kb, diagonal) for hd in range(n_heads)]

    units = []
    for depth in range(n_tiles):
        for t in range(depth, n_tiles):
            units += head_units(t, first_block + t - depth, depth == 0)
    emit(units)

    def body(j, _):
        kb = first_block - 1 - j
        units = []
        for grp in range(n_heads // PV_GROUP):
            for t in range(n_tiles):
                units += [(t, grp * PV_GROUP + g, kb, False) for g in range(PV_GROUP)]
        emit(units)
        return 0

    lax.fori_loop(0, first_block, body, 0)

    for t in range(n_tiles):
        rs = slice(t * tq, (t + 1) * tq)
        for grp in range(n_heads // PV_GROUP):
            sl = slice(grp * PV_WIDTH, (grp + 1) * PV_WIDTH)
            mixed_ref[rs, sl] = (acc_ref[t, grp] * gate_ref[rs, sl].astype(F32)).astype(BF16)
    sb_w = mixed_ref.shape[1]
    o_ref[...] = (x_ref[...] + _dot(mixed_ref[...], w_ref[pool_w:pool_w + sb_w, :])
                  + _dot(pm_ref[:, :pool_w], w_ref[:pool_w, :])
                  + _dot(pm_ref[:, pool_w:], w_ref[pool_w + sb_w:, :]))


def _sb_out(q, k, v, gate, pm, x2d, w_out, pool_w, batch, seq_len):
    rows, sb_w = q.shape
    d = x2d.shape[1]
    n_heads = sb_w // HEAD_DIM
    step_rows = TQ * Q_TILES
    assert seq_len % step_rows == 0
    steps = seq_len // step_rows
    k3 = k.reshape(batch, seq_len, sb_w)
    v3 = v.reshape(batch, seq_len, sb_w)
    row_spec = lambda width: pl.BlockSpec((step_rows, width), lambda b, i: (b * steps + i, 0))
    seq_spec = pl.BlockSpec((None, seq_len, sb_w), lambda b, i: (b, 0, 0))
    full = lambda a: pl.BlockSpec(a.shape, lambda b, i: (0, 0))
    scratch = [((TQ, TQ), BF16),
               ((Q_TILES, n_heads, TQ, LANES), BF16),
               ((seq_len // TQ, PV_GROUP * TQ, sb_w), BF16),
               ((Q_TILES, n_heads // PV_GROUP, TQ, PV_WIDTH), F32),
               ((Q_TILES, n_heads, TQ, 1), F32),
               ((step_rows, sb_w), BF16)]
    carry_in_vmem = ((Q_TILES, n_heads, TQ, LANES), F32)
    vmem = _vmem_limit(
        streamed=2 * [((step_rows, sb_w), BF16)] + [((step_rows, pm.shape[1]), BF16)]
        + 2 * [((seq_len, sb_w), BF16)] + 2 * [((step_rows, d), F32)],
        resident=[(w_out.shape, BF16)],
        scratch=scratch + [carry_in_vmem])
    return pl.pallas_call(
        functools.partial(_sb_out_kernel, n_heads, pool_w),
        out_shape=jax.ShapeDtypeStruct((rows, d), F32),
        grid=(batch, steps),
        in_specs=[row_spec(sb_w), seq_spec, seq_spec, row_spec(sb_w), row_spec(pm.shape[1]),
                  row_spec(d), full(w_out)],
        out_specs=row_spec(d),
        scratch_shapes=[pltpu.VMEM(*s) for s in scratch],
        compiler_params=pltpu.CompilerParams(dimension_semantics=("arbitrary", "arbitrary"),
                                             vmem_limit_bytes=vmem),
        name="sb_out",
    )(q, k3, v3, gate, pm, x2d, w_out)


def kernel(x, mem, norm_g, w_in, pool_w, pool_scale, mem_norm_g, w_mem_kv, q_norm_g, k_norm_g, w_out):
    batch, seq_len, d = x.shape
    mem_len = mem.shape[1]
    depth = norm_g.shape[0]
    pool_width = pool_scale.shape[1]
    assert seq_len % TQ == 0
    x2d = x.reshape(batch * seq_len, d)
    for l in range(depth):
        mem_k, mem_v, w_in_b, w_out_b = _mem_kv(mem.reshape(batch * mem_len, d), mem_norm_g[l][None],
                                                w_mem_kv[l], k_norm_g[l][None], w_in[l], w_out[l])
        q, k, v, gate, pm = _in_proj(x2d, norm_g[l][None], w_in_b, pool_w[l],
                                     pool_scale[l][None], mem_k, mem_v, q_norm_g[l][None],
                                     seq_len, mem_len)
        x2d = _sb_out(q, k, v, gate, pm, x2d, w_out_b, pool_width, batch, seq_len)
    return x2d.reshape(batch, seq_len, d)
```

```python
import functools

import jax
import jax.numpy as jnp
from jax import lax
from jax.experimental import pallas as pl
from jax.experimental.pallas import tpu as pltpu

F32 = jnp.float32
BF16 = jnp.bfloat16

EPS = 1e-6
HEAD_DIM = 64
POOL_WINDOWS = (2, 4, 8, 16)
MAX_WINDOW = max(POOL_WINDOWS)
MEM_HEADS = 4
LANES = 128
HEADS_PER_LANE_TILE = LANES // HEAD_DIM
PV_GROUP = HEADS_PER_LANE_TILE
PV_WIDTH = PV_GROUP * HEAD_DIM

TM = 512
TM_SUB = 256
TQ = 256
Q_TILES = 4
TO = 1024
V7X_VMEM_BYTES = 64 * 1024 * 1024
TEMPORARIES_BYTES = 8 * 1024 * 1024


def _nbytes(shape, dtype):
    n = jnp.dtype(dtype).itemsize
    for s in shape:
        n *= s
    return n


def _vmem_limit(streamed, resident, scratch):
    total = (2 * sum(_nbytes(*b) for b in streamed) + sum(_nbytes(*b) for b in resident)
             + sum(_nbytes(*b) for b in scratch) + TEMPORARIES_BYTES)
    assert total <= V7X_VMEM_BYTES, total
    return total


def _silu(g):
    return g / (1.0 + jnp.exp(-g))


def _dot(a, b):
    return jnp.dot(a, b, preferred_element_type=F32)


def _dot_nt(a, b):
    return lax.dot_general(a, b, (((1,), (1,)), ((), ())), preferred_element_type=F32)


def _per_head(g):
    return jnp.concatenate([g] * MEM_HEADS, axis=1)


def _block_diag(w_ref):
    g, c, _ = w_ref.shape
    zero = jnp.zeros((c, c), F32)
    rows = [jnp.concatenate([w_ref[i] if j == i else zero for j in range(g)], axis=1) for i in range(g)]
    return jnp.concatenate(rows, axis=0).astype(BF16)


def _segment_mean_matrix(width, seg):
    r = lax.broadcasted_iota(jnp.int32, (width, width), 0) // seg
    c = lax.broadcasted_iota(jnp.int32, (width, width), 1) // seg
    return jnp.where(r == c, 1.0 / seg, 0.0).astype(BF16)


def _mem_kv_kernel(mem_ref, g_ref, w_ref, kg_ref, w_in_ref, w_out_ref,
                   k_out, v_out, w_in_out, w_out_out):
    w_in_out[...] = w_in_ref[...].astype(BF16)
    w_out_out[...] = w_out_ref[...].astype(BF16)

    m = mem_ref[...]
    ms = jnp.mean(m * m, axis=-1, keepdims=True)
    h = (m * lax.rsqrt(ms + EPS) * g_ref[...]).astype(BF16)
    kv = _dot(h, w_ref[...].astype(BF16))
    width = kv.shape[1] // 2
    k = kv[:, :width]
    v = kv[:, width:]
    seg = _segment_mean_matrix(width, width // MEM_HEADS)
    kms = _dot((k * k).astype(BF16), seg)
    k_out[...] = (k * lax.rsqrt(kms + EPS) * _per_head(kg_ref[...])).astype(BF16)
    v_out[...] = v.astype(BF16)


def _mem_kv(mem2d, mem_norm_g, w_mem_kv, k_norm_g, w_in, w_out):
    rows, d = mem2d.shape
    width = w_mem_kv.shape[1] // 2
    steps = 4
    assert rows % steps == 0 and w_in.shape[0] % steps == 0 and w_out.shape[0] % steps == 0
    chunk = lambda a: pl.BlockSpec((a.shape[0] // steps, a.shape[1]), lambda i: (i, 0))
    full = lambda a: pl.BlockSpec(a.shape, lambda i: (0, 0))
    kv_sd = jax.ShapeDtypeStruct((rows, width), BF16)
    kv_spec = pl.BlockSpec((rows // steps, width), lambda i: (i, 0))
    chunk_shape = lambda a: (a.shape[0] // steps, a.shape[1])
    vmem = _vmem_limit(
        streamed=[(chunk_shape(mem2d), F32), (chunk_shape(w_in), F32), (chunk_shape(w_out), F32),
                  (chunk_shape(w_in), BF16), (chunk_shape(w_out), BF16)] + 2 * [(kv_spec.block_shape, BF16)],
        resident=[(w_mem_kv.shape, F32), (mem_norm_g.shape, F32), (k_norm_g.shape, F32)],
        scratch=[])
    return pl.pallas_call(
        _mem_kv_kernel,
        out_shape=(kv_sd, kv_sd, jax.ShapeDtypeStruct(w_in.shape, BF16),
                   jax.ShapeDtypeStruct(w_out.shape, BF16)),
        grid=(steps,),
        in_specs=[chunk(mem2d), full(mem_norm_g), full(w_mem_kv), full(k_norm_g),
                  chunk(w_in), chunk(w_out)],
        out_specs=(kv_spec, kv_spec, chunk(w_in), chunk(w_out)),
        compiler_params=pltpu.CompilerParams(dimension_semantics=("arbitrary",),
                                             vmem_limit_bytes=vmem),
        name="mem_kv",
    )(mem2d, mem_norm_g, w_mem_kv, k_norm_g, w_in, w_out)


DONE, MIX_READY = object(), object()

def _in_proj_kernel(seq_len, pool_w, sb_w, mem_w,
                    x_ref, g_ref, w_ref, pw_ref, pscale_ref, mk_ref, mv_ref, qg_ref,
                    q_out, k_out, v_out, gate_out, pm_out, pbuf, halo_ref):
    tm = x_ref.shape[0]
    n_sub = pbuf.shape[0]
    rows = tm // n_sub
    t0 = (pl.program_id(0) * tm) % seq_len
    head_w = mem_w // MEM_HEADS

    @pl.when(t0 == 0)
    def _():
        halo_ref[...] = jnp.zeros(halo_ref.shape, F32)

    lane = lax.broadcasted_iota(jnp.int32, (rows, pool_w), 1)
    group = lane // (pool_w // len(POOL_WINDOWS))
    row_iota = lax.broadcasted_iota(jnp.int32, (rows, pool_w), 0)
    seg = _segment_mean_matrix(mem_w, head_w)
    q_gain = _per_head(qg_ref[...])
    pool_mix = _block_diag(pw_ref)
    lane_q = lax.broadcasted_iota(jnp.int32, (rows, mem_w), 1) // head_w
    mv = mv_ref[...].astype(F32)
    lane_v = lax.broadcasted_iota(jnp.int32, mv.shape, 1) // head_w
    mv_heads = [jnp.where(lane_v == hd, mv, 0.0).astype(BF16) for hd in range(MEM_HEADS)]

    h, kept = {}, {}

    def normalize(s, n_chunks=4):
        chunk = rows // n_chunks
        parts = []
        for i in range(n_chunks):
            x = x_ref[s * rows + i * chunk:s * rows + (i + 1) * chunk, :]
            ms = jnp.mean(x * x, axis=-1, keepdims=True)
            parts.append((x * lax.rsqrt(ms + EPS) * g_ref[...]).astype(BF16))
            if i + 1 < n_chunks:
                yield
        h[s] = jnp.concatenate(parts, axis=0)

    def project(s):
        hs = h.pop(s)
        rs = slice(s * rows, (s + 1) * rows)

        def proj(c0, width):
            return _dot(hs, w_ref[:, c0:c0 + width])

        c_pool, c_sb, c_mem = 0, 2 * pool_w, 2 * pool_w + 4 * sb_w
        pool_v = proj(c_pool, pool_w)
        yield
        pool_g = proj(c_pool + pool_w, pool_w)
        yield
        mem_q = proj(c_mem, mem_w)
        yield
        mem_g = proj(c_mem + mem_w, mem_w)
        kept[s] = (pool_v, pool_g, mem_q, mem_g)
        yield MIX_READY
        q_out[rs, :] = (proj(c_sb, sb_w) * HEAD_DIM ** -0.5).astype(BF16)
        yield
        k_out[rs, :] = proj(c_sb + sb_w, sb_w).astype(BF16)
        yield
        v_out[rs, :] = proj(c_sb + 2 * sb_w, sb_w).astype(BF16)
        yield
        gate_out[rs, :] = _silu(proj(c_sb + 3 * sb_w, sb_w)).astype(BF16)

    def mix(s):
        pool_v, pool_g, mem_q, mem_g = kept.pop(s)
        rs = slice(s * rows, (s + 1) * rows)

        pbuf[s, 0:MAX_WINDOW, :] = halo_ref[...]
        pbuf[s, MAX_WINDOW:MAX_WINDOW + rows, :] = pool_v
        halo_ref[...] = pool_v[rows - MAX_WINDOW:, :]
        t_seq = t0 + s * rows + row_iota
        win_sum = pool_v
        window = jnp.full((rows, pool_w), 1, jnp.int32)
        acc = pool_v
        shift = 1
        for g, w in enumerate(POOL_WINDOWS):
            while shift < w:
                acc = acc + pbuf[s, MAX_WINDOW - shift:MAX_WINDOW - shift + rows, :]
                shift += 1
            win_sum = jnp.where(group == g, acc, win_sum)
            window = jnp.where(group == g, w, window)
        count = jnp.minimum(t_seq + 1, window).astype(F32)
        pooled = (win_sum / count - pool_v).astype(BF16)
        y_pool = _dot(pooled, pool_mix)
        yield

        qms = _dot((mem_q * mem_q).astype(BF16), seg)
        yield
        pm_out[rs, 0:pool_w] = (y_pool * pscale_ref[...] * _silu(pool_g)).astype(BF16)
        qn = mem_q * lax.rsqrt(qms + EPS) * q_gain
        qn = qn * (head_w ** -0.5)

        scores = []
        for hd in range(MEM_HEADS):
            scores.append(_dot_nt(jnp.where(lane_q == hd, qn, 0.0).astype(BF16), mk_ref[...]))
            yield
        y_mem = None
        for hd in range(MEM_HEADS):
            e = jnp.exp(scores[hd] - jnp.max(scores[hd], axis=-1, keepdims=True))
            pv = _dot(e.astype(BF16), mv_heads[hd])
            yield
            pv = pv * (1.0 / jnp.sum(e, axis=-1, keepdims=True))
            y_mem = pv if y_mem is None else y_mem + pv
        pm_out[rs, pool_w:pool_w + mem_w] = (y_mem * _silu(mem_g)).astype(BF16)

    def sub_tile(s):
        yield from normalize(s)
        yield from project(s)

    stages = [(0, sub_tile(0))]
    while stages:
        for entry in list(stages):
            s, stage = entry
            event = next(stage, DONE)
            if event is DONE:
                stages.remove(entry)
            elif event is MIX_READY:
                stages.append((s, mix(s)))
                if s + 1 < n_sub:
                    stages.append((s + 1, sub_tile(s + 1)))


def _in_proj(x2d, norm_g, w_in, pool_w3, pool_scale, mem_k, mem_v, q_norm_g, seq_len, mem_len):
    rows, d = x2d.shape
    pool_w = pool_w3.shape[0] * pool_w3.shape[1]
    mem_w = mem_k.shape[1]
    sb_w = (w_in.shape[1] - 2 * pool_w - 2 * mem_w) // 4
    assert seq_len % TM == 0 and rows % TM == 0
    tiles_per_seq = seq_len // TM
    row_spec = lambda width: pl.BlockSpec((TM, width), lambda i: (i, 0))
    full = lambda a: pl.BlockSpec(a.shape, lambda i: (0,) * a.ndim)
    mem_spec = pl.BlockSpec((mem_len, mem_w), lambda i: (i // tiles_per_seq, 0))
    out_sd = lambda width: jax.ShapeDtypeStruct((rows, width), BF16)
    scratch = [((TM // TM_SUB, MAX_WINDOW + TM_SUB, pool_w), F32), ((MAX_WINDOW, pool_w), F32)]
    vmem = _vmem_limit(
        streamed=[((TM, d), F32)] + 4 * [((TM, sb_w), BF16)] + [((TM, pool_w + mem_w), BF16)]
        + 2 * [((mem_len, mem_w), BF16)],
        resident=[(w_in.shape, BF16), (pool_w3.shape, F32), (norm_g.shape, F32),
                  (pool_scale.shape, F32), (q_norm_g.shape, F32)],
        scratch=scratch)
    return pl.pallas_call(
        functools.partial(_in_proj_kernel, seq_len, pool_w, sb_w, mem_w),
        out_shape=(out_sd(sb_w), out_sd(sb_w), out_sd(sb_w), out_sd(sb_w), out_sd(pool_w + mem_w)),
        grid=(rows // TM,),
        in_specs=[row_spec(d), full(norm_g), full(w_in), full(pool_w3), full(pool_scale),
                  mem_spec, mem_spec, full(q_norm_g)],
        out_specs=(row_spec(sb_w), row_spec(sb_w), row_spec(sb_w), row_spec(sb_w),
                   row_spec(pool_w + mem_w)),
        scratch_shapes=[pltpu.VMEM(*s) for s in scratch],
        compiler_params=pltpu.CompilerParams(dimension_semantics=("arbitrary",),
                                             vmem_limit_bytes=vmem),
        name="in_proj",
    )(x2d, norm_g, w_in, pool_w3, pool_scale, mem_k, mem_v, q_norm_g)


def _sb_kernel(n_heads,
               q_ref, k_ref, v_ref, gate_ref,
               o_ref, tri_ref, qm_ref, vm_ref, acc_ref, carry_ref):
    tq = tri_ref.shape[0]
    n_tiles = q_ref.shape[0] // tq
    n_pairs = n_heads // HEADS_PER_LANE_TILE
    first_block = pl.program_id(1) * n_tiles
    half = lax.broadcasted_iota(jnp.int32, (1, LANES), 1) // HEAD_DIM

    @pl.when(pl.program_id(1) == 0)
    def _():
        lane_head = (lax.broadcasted_iota(jnp.int32, (tq, v_ref.shape[1]), 1) // HEAD_DIM) % PV_GROUP
        for kb in range(v_ref.shape[0] // tq):
            v = v_ref[kb * tq:(kb + 1) * tq, :].astype(F32)
            for g in range(PV_GROUP):
                vm_ref[kb, g * tq:(g + 1) * tq, :] = jnp.where(lane_head == g, v, 0.0).astype(BF16)

    for t in range(n_tiles):
        for p in range(n_pairs):
            qp = q_ref[t * tq:(t + 1) * tq, p * LANES:(p + 1) * LANES].astype(F32)
            for hh in range(HEADS_PER_LANE_TILE):
                qm_ref[t, p * HEADS_PER_LANE_TILE + hh] = jnp.where(half == hh, qp, 0.0).astype(BF16)

    row = lax.broadcasted_iota(jnp.int32, (tq, tq), 0)
    col = lax.broadcasted_iota(jnp.int32, (tq, tq), 1)
    causal = col < row
    tri_ref[...] = jnp.where(row >= col, -1.0, 0.0).astype(BF16)

    def emit(units):
        logit, suffix, a = {}, {}, {}

        def logits(u):
            t, hd, kb, _ = units[u]
            p = hd // HEADS_PER_LANE_TILE
            k0 = pl.multiple_of(kb * tq, tq)
            logit[u] = _dot_nt(qm_ref[t, hd], k_ref[pl.ds(k0, tq), p * LANES:(p + 1) * LANES])

        def suffix_sum(u):
            diagonal = units[u][3]
            z = logit[u].astype(BF16)
            sp = jnp.maximum(z, 0.0) + jnp.log(1.0 + jnp.exp(-jnp.abs(z)))
            if diagonal:
                sp = jnp.where(causal, sp, 0.0)
            suffix[u] = _dot(sp, tri_ref[...])

        def weights(u):
            t, hd, _, diagonal = units[u]
            log_a = logit.pop(u) + suffix[u]
            if not diagonal:
                log_a = log_a + carry_ref[t, hd]
            w = jnp.exp(log_a)
            if diagonal:
                w = jnp.where(causal, w, 0.0)
                carry_ref[t, hd] = suffix.pop(u)[:, 0:1]
            else:
                carry_ref[t, hd] = carry_ref[t, hd] + suffix.pop(u)[:, 0:1]
            a[u] = w.astype(BF16)

        def weighted_values(u_last):
            t, hd, kb, diagonal = units[u_last]
            grp = hd // PV_GROUP
            a_cat = jnp.concatenate([a.pop(u) for u in range(u_last - PV_GROUP + 1, u_last + 1)], axis=1)
            out = _dot(a_cat, vm_ref[kb, :, grp * PV_WIDTH:(grp + 1) * PV_WIDTH])
            if diagonal:
                acc_ref[t, grp] = out
            else:
                acc_ref[t, grp] = acc_ref[t, grp] + out

        n = len(units)
        for step in range(n + 2):
            if step < n:
                logits(step)
            if 0 <= step - 1 < n:
                suffix_sum(step - 1)
            if 0 <= step - 2 < n:
                weights(step - 2)
                if units[step - 2][1] % PV_GROUP == PV_GROUP - 1:
                    weighted_values(step - 2)

    def head_units(t, kb, diagonal):
        return [(t, hd, kb, diagonal) for hd in range(n_heads)]

    units = []
    for depth in range(n_tiles):
        for t in range(depth, n_tiles):
            units += head_units(t, first_block + t - depth, depth == 0)
    emit(units)

    def body(j, _):
        kb = first_block - 1 - j
        units = []
        for grp in range(n_heads // PV_GROUP):
            for t in range(n_tiles):
                units += [(t, grp * PV_GROUP + g, kb, False) for g in range(PV_GROUP)]
        emit(units)
        return 0

    lax.fori_loop(0, first_block, body, 0)

    for t in range(n_tiles):
        rs = slice(t * tq, (t + 1) * tq)
        for grp in range(n_heads // PV_GROUP):
            sl = slice(grp * PV_WIDTH, (grp + 1) * PV_WIDTH)
            o_ref[rs, sl] = (acc_ref[t, grp] * gate_ref[rs, sl].astype(F32)).astype(BF16)


def _sb(q, k, v, gate, batch, seq_len):
    rows, sb_w = q.shape
    n_heads = sb_w // HEAD_DIM
    step_rows = TQ * Q_TILES
    assert seq_len % step_rows == 0
    steps = seq_len // step_rows
    k3 = k.reshape(batch, seq_len, sb_w)
    v3 = v.reshape(batch, seq_len, sb_w)
    row_spec = lambda width: pl.BlockSpec((step_rows, width), lambda b, i: (b * steps + i, 0))
    seq_spec = pl.BlockSpec((None, seq_len, sb_w), lambda b, i: (b, 0, 0))
    scratch = [((TQ, TQ), BF16),
               ((Q_TILES, n_heads, TQ, LANES), BF16),
               ((seq_len // TQ, PV_GROUP * TQ, sb_w), BF16),
               ((Q_TILES, n_heads // PV_GROUP, TQ, PV_WIDTH), F32),
               ((Q_TILES, n_heads, TQ, 1), F32)]
    carry_in_vmem = ((Q_TILES, n_heads, TQ, LANES), F32)
    vmem = _vmem_limit(
        streamed=3 * [((step_rows, sb_w), BF16)] + 2 * [((seq_len, sb_w), BF16)],
        resident=[],
        scratch=scratch + [carry_in_vmem])
    return pl.pallas_call(
        functools.partial(_sb_kernel, n_heads),
        out_shape=jax.ShapeDtypeStruct((rows, sb_w), BF16),
        grid=(batch, steps),
        in_specs=[row_spec(sb_w), seq_spec, seq_spec, row_spec(sb_w)],
        out_specs=row_spec(sb_w),
        scratch_shapes=[pltpu.VMEM(*s) for s in scratch],
        compiler_params=pltpu.CompilerParams(dimension_semantics=("arbitrary", "arbitrary"),
                                             vmem_limit_bytes=vmem),
        name="sb",
    )(q, k3, v3, gate)


def _out_proj_kernel(pool_w, sb_ref, pm_ref, x_ref, w_ref, o_ref):
    sb_w = sb_ref.shape[1]
    o_ref[...] = (x_ref[...] + _dot(sb_ref[...], w_ref[pool_w:pool_w + sb_w, :])
                  + _dot(pm_ref[:, :pool_w], w_ref[:pool_w, :])
                  + _dot(pm_ref[:, pool_w:], w_ref[pool_w + sb_w:, :]))


def _out_proj(y_sb, pm, x2d, w_out, pool_w):
    rows, d = x2d.shape
    assert rows % TO == 0
    row_spec = lambda width: pl.BlockSpec((TO, width), lambda i: (i, 0))
    vmem = _vmem_limit(
        streamed=[((TO, y_sb.shape[1]), BF16), ((TO, pm.shape[1]), BF16)] + 2 * [((TO, d), F32)],
        resident=[(w_out.shape, BF16)],
        scratch=[])
    return pl.pallas_call(
        functools.partial(_out_proj_kernel, pool_w),
        out_shape=jax.ShapeDtypeStruct((rows, d), F32),
        grid=(rows // TO,),
        in_specs=[row_spec(y_sb.shape[1]), row_spec(pm.shape[1]), row_spec(d),
                  pl.BlockSpec(w_out.shape, lambda i: (0, 0))],
        out_specs=row_spec(d),
        compiler_params=pltpu.CompilerParams(dimension_semantics=("arbitrary",),
                                             vmem_limit_bytes=vmem),
        name="out_proj",
    )(y_sb, pm, x2d, w_out)


def kernel(x, mem, norm_g, w_in, pool_w, pool_scale, mem_norm_g, w_mem_kv, q_norm_g, k_norm_g, w_out):
    batch, seq_len, d = x.shape
    mem_len = mem.shape[1]
    depth = norm_g.shape[0]
    pool_width = pool_scale.shape[1]
    assert seq_len % TQ == 0
    x2d = x.reshape(batch * seq_len, d)
    for l in range(depth):
        mem_k, mem_v, w_in_b, w_out_b = _mem_kv(mem.reshape(batch * mem_len, d), mem_norm_g[l][None],
                                                w_mem_kv[l], k_norm_g[l][None], w_in[l], w_out[l])
        q, k, v, gate, pm = _in_proj(x2d, norm_g[l][None], w_in_b, pool_w[l],
                                     pool_scale[l][None], mem_k, mem_v, q_norm_g[l][None],
                                     seq_len, mem_len)
        y_sb = _sb(q, k, v, gate, batch, seq_len)
        x2d = _out_proj(y_sb, pm, x2d, w_out_b, pool_width)
    return x2d.reshape(batch, seq_len, d)
```

```python
import functools

import jax
import jax.numpy as jnp
from jax import lax
from jax.experimental import pallas as pl
from jax.experimental.pallas import tpu as pltpu

F32 = jnp.float32
BF16 = jnp.bfloat16

EPS = 1e-6
HEAD_DIM = 64
POOL_WINDOWS = (2, 4, 8, 16)
MAX_WINDOW = max(POOL_WINDOWS)
MEM_HEADS = 4
LANES = 128
HEADS_PER_LANE_TILE = LANES // HEAD_DIM
PV_GROUP = HEADS_PER_LANE_TILE
PV_WIDTH = PV_GROUP * HEAD_DIM

TM = 512
WEIGHT_CAST_CHUNKS = 8
TM_SUB = 256
TQ = 256
Q_TILES = 4
V7X_VMEM_BYTES = 64 * 1024 * 1024
TEMPORARIES_BYTES = 8 * 1024 * 1024


def _nbytes(shape, dtype):
    n = jnp.dtype(dtype).itemsize
    for s in shape:
        n *= s
    return n


def _vmem_limit(streamed, resident, scratch):
    total = (2 * sum(_nbytes(*b) for b in streamed) + sum(_nbytes(*b) for b in resident)
             + sum(_nbytes(*b) for b in scratch) + TEMPORARIES_BYTES)
    assert total <= V7X_VMEM_BYTES, total
    return total


def _silu(g):
    return g / (1.0 + jnp.exp(-g))


def _dot(a, b):
    return jnp.dot(a, b, preferred_element_type=F32)


def _dot_nt(a, b):
    return lax.dot_general(a, b, (((1,), (1,)), ((), ())), preferred_element_type=F32)


def _per_head(g):
    return jnp.concatenate([g] * MEM_HEADS, axis=1)


def _block_diag(w_ref):
    g, c, _ = w_ref.shape
    zero = jnp.zeros((c, c), F32)
    rows = [jnp.concatenate([w_ref[i] if j == i else zero for j in range(g)], axis=1) for i in range(g)]
    return jnp.concatenate(rows, axis=0).astype(BF16)


def _segment_mean_matrix(width, seg):
    r = lax.broadcasted_iota(jnp.int32, (width, width), 0) // seg
    c = lax.broadcasted_iota(jnp.int32, (width, width), 1) // seg
    return jnp.where(r == c, 1.0 / seg, 0.0).astype(BF16)


DONE, MIX_READY = object(), object()

def _in_proj_kernel(seq_len, pool_w, sb_w, mem_w,
                    x_ref, g_ref, w_in_ref, pw_ref, pscale_ref, mem_ref, mg_ref, w_mem_ref, kg_ref, qg_ref,
                    w_out_ref,
                    q_out, k_out, v_out, gate_out, pm_out, w_out_out,
                    pbuf, halo_ref, w_ref, mk_ref, mv_ref):
    tm = x_ref.shape[0]
    n_sub = pbuf.shape[0]
    rows = tm // n_sub
    t0 = (pl.program_id(0) * tm) % seq_len
    head_w = mem_w // MEM_HEADS

    @pl.when(pl.program_id(0) == 0)
    def _():
        chunk = w_in_ref.shape[0] // WEIGHT_CAST_CHUNKS
        for c in range(WEIGHT_CAST_CHUNKS):
            w_ref[c * chunk:(c + 1) * chunk, :] = w_in_ref[c * chunk:(c + 1) * chunk, :].astype(BF16)
        w_out_out[...] = w_out_ref[...].astype(BF16)

    @pl.when(t0 == 0)
    def _():
        m = mem_ref[...]
        ms = jnp.mean(m * m, axis=-1, keepdims=True)
        hm = (m * lax.rsqrt(ms + EPS) * mg_ref[...]).astype(BF16)
        kv = _dot(hm, w_mem_ref[...].astype(BF16))
        k = kv[:, :mem_w]
        kms = _dot((k * k).astype(BF16), _segment_mean_matrix(mem_w, head_w))
        mk_ref[...] = (k * lax.rsqrt(kms + EPS) * _per_head(kg_ref[...])).astype(BF16)
        mv_ref[...] = kv[:, mem_w:].astype(BF16)

    @pl.when(t0 == 0)
    def _():
        halo_ref[...] = jnp.zeros(halo_ref.shape, F32)

    lane = lax.broadcasted_iota(jnp.int32, (rows, pool_w), 1)
    group = lane // (pool_w // len(POOL_WINDOWS))
    row_iota = lax.broadcasted_iota(jnp.int32, (rows, pool_w), 0)
    seg = _segment_mean_matrix(mem_w, head_w)
    q_gain = _per_head(qg_ref[...])
    pool_mix = _block_diag(pw_ref)
    lane_q = lax.broadcasted_iota(jnp.int32, (rows, mem_w), 1) // head_w
    mv = mv_ref[...].astype(F32)
    lane_v = lax.broadcasted_iota(jnp.int32, mv.shape, 1) // head_w
    mv_heads = [jnp.where(lane_v == hd, mv, 0.0).astype(BF16) for hd in range(MEM_HEADS)]

    h, kept = {}, {}

    def normalize(s, n_chunks=4):
        chunk = rows // n_chunks
        parts = []
        for i in range(n_chunks):
            x = x_ref[s * rows + i * chunk:s * rows + (i + 1) * chunk, :]
            ms = jnp.mean(x * x, axis=-1, keepdims=True)
            parts.append((x * lax.rsqrt(ms + EPS) * g_ref[...]).astype(BF16))
            if i + 1 < n_chunks:
                yield
        h[s] = jnp.concatenate(parts, axis=0)

    def project(s):
        hs = h.pop(s)
        rs = slice(s * rows, (s + 1) * rows)

        def proj(c0, width):
            return _dot(hs, w_ref[:, c0:c0 + width])

        c_pool, c_sb, c_mem = 0, 2 * pool_w, 2 * pool_w + 4 * sb_w
        pool_v = proj(c_pool, pool_w)
        yield
        pool_g = proj(c_pool + pool_w, pool_w)
        yield
        mem_q = proj(c_mem, mem_w)
        yield
        mem_g = proj(c_mem + mem_w, mem_w)
        kept[s] = (pool_v, pool_g, mem_q, mem_g)
        yield MIX_READY
        q_out[rs, :] = (proj(c_sb, sb_w) * HEAD_DIM ** -0.5).astype(BF16)
        yield
        k_out[rs, :] = proj(c_sb + sb_w, sb_w).astype(BF16)
        yield
        v_out[rs, :] = proj(c_sb + 2 * sb_w, sb_w).astype(BF16)
        yield
        gate_out[rs, :] = _silu(proj(c_sb + 3 * sb_w, sb_w)).astype(BF16)

    def mix(s):
        pool_v, pool_g, mem_q, mem_g = kept.pop(s)
        rs = slice(s * rows, (s + 1) * rows)

        pbuf[s, 0:MAX_WINDOW, :] = halo_ref[...]
        pbuf[s, MAX_WINDOW:MAX_WINDOW + rows, :] = pool_v
        halo_ref[...] = pool_v[rows - MAX_WINDOW:, :]
        t_seq = t0 + s * rows + row_iota
        win_sum = pool_v
        window = jnp.full((rows, pool_w), 1, jnp.int32)
        acc = pool_v
        shift = 1
        for g, w in enumerate(POOL_WINDOWS):
            while shift < w:
                acc = acc + pbuf[s, MAX_WINDOW - shift:MAX_WINDOW - shift + rows, :]
                shift += 1
            win_sum = jnp.where(group == g, acc, win_sum)
            window = jnp.where(group == g, w, window)
        count = jnp.minimum(t_seq + 1, window).astype(F32)
        pooled = (win_sum / count - pool_v).astype(BF16)
        y_pool = _dot(pooled, pool_mix)
        yield

        qms = _dot((mem_q * mem_q).astype(BF16), seg)
        yield
        pm_out[rs, 0:pool_w] = (y_pool * pscale_ref[...] * _silu(pool_g)).astype(BF16)
        qn = mem_q * lax.rsqrt(qms + EPS) * q_gain
        qn = qn * (head_w ** -0.5)

        scores = []
        for hd in range(MEM_HEADS):
            scores.append(_dot_nt(jnp.where(lane_q == hd, qn, 0.0).astype(BF16), mk_ref[...]))
            yield
        y_mem = None
        for hd in range(MEM_HEADS):
            e = jnp.exp(scores[hd] - jnp.max(scores[hd], axis=-1, keepdims=True))
            pv = _dot(e.astype(BF16), mv_heads[hd])
            yield
            pv = pv * (1.0 / jnp.sum(e, axis=-1, keepdims=True))
            y_mem = pv if y_mem is None else y_mem + pv
        pm_out[rs, pool_w:pool_w + mem_w] = (y_mem * _silu(mem_g)).astype(BF16)

    def sub_tile(s):
        yield from normalize(s)
        yield from project(s)

    stages = [(0, sub_tile(0))]
    while stages:
        for entry in list(stages):
            s, stage = entry
            event = next(stage, DONE)
            if event is DONE:
                stages.remove(entry)
            elif event is MIX_READY:
                stages.append((s, mix(s)))
                if s + 1 < n_sub:
                    stages.append((s + 1, sub_tile(s + 1)))


def _in_proj(x2d, norm_g, w_in, pool_w3, pool_scale, mem2d, mem_norm_g, w_mem_kv, k_norm_g, q_norm_g, w_out,
             seq_len, mem_len):
    rows, d = x2d.shape
    pool_w = pool_w3.shape[0] * pool_w3.shape[1]
    mem_w = w_mem_kv.shape[1] // 2
    sb_w = (w_in.shape[1] - 2 * pool_w - 2 * mem_w) // 4
    assert seq_len % TM == 0 and rows % TM == 0
    tiles_per_seq = seq_len // TM
    row_spec = lambda width: pl.BlockSpec((TM, width), lambda i: (i, 0))
    full = lambda a: pl.BlockSpec(a.shape, lambda i: (0,) * a.ndim)
    mem_spec = pl.BlockSpec((mem_len, d), lambda i: (i // tiles_per_seq, 0))
    out_sd = lambda width: jax.ShapeDtypeStruct((rows, width), BF16)
    scratch = [((TM // TM_SUB, MAX_WINDOW + TM_SUB, pool_w), F32), ((MAX_WINDOW, pool_w), F32),
               (w_in.shape, BF16), ((mem_len, mem_w), BF16), ((mem_len, mem_w), BF16)]
    vmem = _vmem_limit(
        streamed=[((TM, d), F32)] + 4 * [((TM, sb_w), BF16)] + [((TM, pool_w + mem_w), BF16)]
        + [((mem_len, d), F32)],
        resident=[(w_in.shape, F32), (w_out.shape, F32), (w_out.shape, BF16), (w_mem_kv.shape, F32),
                  (pool_w3.shape, F32), (norm_g.shape, F32), (pool_scale.shape, F32),
                  (mem_norm_g.shape, F32), (k_norm_g.shape, F32), (q_norm_g.shape, F32)],
        scratch=scratch)
    return pl.pallas_call(
        functools.partial(_in_proj_kernel, seq_len, pool_w, sb_w, mem_w),
        out_shape=(out_sd(sb_w), out_sd(sb_w), out_sd(sb_w), out_sd(sb_w), out_sd(pool_w + mem_w),
                   jax.ShapeDtypeStruct(w_out.shape, BF16)),
        grid=(rows // TM,),
        in_specs=[row_spec(d), full(norm_g), full(w_in), full(pool_w3), full(pool_scale),
                  mem_spec, full(mem_norm_g), full(w_mem_kv), full(k_norm_g), full(q_norm_g), full(w_out)],
        out_specs=(row_spec(sb_w), row_spec(sb_w), row_spec(sb_w), row_spec(sb_w),
                   row_spec(pool_w + mem_w), full(w_out)),
        scratch_shapes=[pltpu.VMEM(*s) for s in scratch],
        compiler_params=pltpu.CompilerParams(dimension_semantics=("arbitrary",),
                                             vmem_limit_bytes=vmem),
        name="in_proj",
    )(x2d, norm_g, w_in, pool_w3, pool_scale, mem2d, mem_norm_g, w_mem_kv, k_norm_g, q_norm_g, w_out)


def _sb_out_kernel(n_heads, pool_w,
                   q_ref, k_ref, v_ref, gate_ref, pm_ref, x_ref, w_ref,
                   o_ref, tri_ref, qm_ref, vm_ref, acc_ref, carry_ref, mixed_ref):
    tq = tri_ref.shape[0]
    n_tiles = q_ref.shape[0] // tq
    n_pairs = n_heads // HEADS_PER_LANE_TILE
    first_block = pl.program_id(1) * n_tiles
    half = lax.broadcasted_iota(jnp.int32, (1, LANES), 1) // HEAD_DIM

    @pl.when(pl.program_id(1) == 0)
    def _():
        lane_head = (lax.broadcasted_iota(jnp.int32, (tq, v_ref.shape[1]), 1) // HEAD_DIM) % PV_GROUP
        for kb in range(v_ref.shape[0] // tq):
            v = v_ref[kb * tq:(kb + 1) * tq, :].astype(F32)
            for g in range(PV_GROUP):
                vm_ref[kb, g * tq:(g + 1) * tq, :] = jnp.where(lane_head == g, v, 0.0).astype(BF16)

    for t in range(n_tiles):
        for p in range(n_pairs):
            qp = q_ref[t * tq:(t + 1) * tq, p * LANES:(p + 1) * LANES].astype(F32)
            for hh in range(HEADS_PER_LANE_TILE):
                qm_ref[t, p * HEADS_PER_LANE_TILE + hh] = jnp.where(half == hh, qp, 0.0).astype(BF16)

    row = lax.broadcasted_iota(jnp.int32, (tq, tq), 0)
    col = lax.broadcasted_iota(jnp.int32, (tq, tq), 1)
    causal = col < row
    tri_ref[...] = jnp.where(row >= col, -1.0, 0.0).astype(BF16)

    def emit(units):
        logit, suffix, a = {}, {}, {}

        def logits(u):
            t, hd, kb, _ = units[u]
            p = hd // HEADS_PER_LANE_TILE
            k0 = pl.multiple_of(kb * tq, tq)
            logit[u] = _dot_nt(qm_ref[t, hd], k_ref[pl.ds(k0, tq), p * LANES:(p + 1) * LANES])

        def suffix_sum(u):
            diagonal = units[u][3]
            z = logit[u].astype(BF16)
            sp = jnp.maximum(z, 0.0) + jnp.log(1.0 + jnp.exp(-jnp.abs(z)))
            if diagonal:
                sp = jnp.where(causal, sp, 0.0)
            suffix[u] = _dot(sp, tri_ref[...])

        def weights(u):
            t, hd, _, diagonal = units[u]
            log_a = logit.pop(u) + suffix[u]
            if not diagonal:
                log_a = log_a + carry_ref[t, hd]
            w = jnp.exp(log_a)
            if diagonal:
                w = jnp.where(causal, w, 0.0)
                carry_ref[t, hd] = suffix.pop(u)[:, 0:1]
            else:
                carry_ref[t, hd] = carry_ref[t, hd] + suffix.pop(u)[:, 0:1]
            a[u] = w.astype(BF16)

        def weighted_values(u_last):
            t, hd, kb, diagonal = units[u_last]
            grp = hd // PV_GROUP
            a_cat = jnp.concatenate([a.pop(u) for u in range(u_last - PV_GROUP + 1, u_last + 1)], axis=1)
            out = _dot(a_cat, vm_ref[kb, :, grp * PV_WIDTH:(grp + 1) * PV_WIDTH])
            if diagonal:
                acc_ref[t, grp] = out
            else:
                acc_ref[t, grp] = acc_ref[t, grp] + out

        n = len(units)
        for step in range(n + 2):
            if step < n:
                logits(step)
            if 0 <= step - 1 < n:
                suffix_sum(step - 1)
            if 0 <= step - 2 < n:
                weights(step - 2)
                if units[step - 2][1] % PV_GROUP == PV_GROUP - 1:
                    weighted_values(step - 2)

    def head_units(t, kb, diagonal):
        return [(t, hd, kb, diagonal) for hd in range(n_heads)]

    units = []
    for depth in range(n_tiles):
        for t in range(depth, n_tiles):
            units += head_units(t, first_block + t - depth, depth == 0)
    emit(units)

    def body(j, _):
        kb = first_block - 1 - j
        units = []
        for grp in range(n_heads // PV_GROUP):
            for t in range(n_tiles):
                units += [(t, grp * PV_GROUP + g, kb, False) for g in range(PV_GROUP)]
        emit(units)
        return 0

    lax.fori_loop(0, first_block, body, 0)

    for t in range(n_tiles):
        rs = slice(t * tq, (t + 1) * tq)
        for grp in range(n_heads // PV_GROUP):
            sl = slice(grp * PV_WIDTH, (grp + 1) * PV_WIDTH)
            mixed_ref[rs, sl] = (acc_ref[t, grp] * gate_ref[rs, sl].astype(F32)).astype(BF16)
    sb_w = mixed_ref.shape[1]
    o_ref[...] = (x_ref[...] + _dot(mixed_ref[...], w_ref[pool_w:pool_w + sb_w, :])
                  + _dot(pm_ref[:, :pool_w], w_ref[:pool_w, :])
                  + _dot(pm_ref[:, pool_w:], w_ref[pool_w + sb_w:, :]))


def _sb_out(q, k, v, gate, pm, x2d, w_out, pool_w, batch, seq_len):
    rows, sb_w = q.shape
    d = x2d.shape[1]
    n_heads = sb_w // HEAD_DIM
    step_rows = TQ * Q_TILES
    assert seq_len % step_rows == 0
    steps = seq_len // step_rows
    k3 = k.reshape(batch, seq_len, sb_w)
    v3 = v.reshape(batch, seq_len, sb_w)
    row_spec = lambda width: pl.BlockSpec((step_rows, width), lambda b, i: (b * steps + i, 0))
    seq_spec = pl.BlockSpec((None, seq_len, sb_w), lambda b, i: (b, 0, 0))
    full = lambda a: pl.BlockSpec(a.shape, lambda b, i: (0, 0))
    scratch = [((TQ, TQ), BF16),
               ((Q_TILES, n_heads, TQ, LANES), BF16),
               ((seq_len // TQ, PV_GROUP * TQ, sb_w), BF16),
               ((Q_TILES, n_heads // PV_GROUP, TQ, PV_WIDTH), F32),
               ((Q_TILES, n_heads, TQ, 1), F32),
               ((step_rows, sb_w), BF16)]
    carry_in_vmem = ((Q_TILES, n_heads, TQ, LANES), F32)
    vmem = _vmem_limit(
        streamed=2 * [((step_rows, sb_w), BF16)] + [((step_rows, pm.shape[1]), BF16)]
        + 2 * [((seq_len, sb_w), BF16)] + 2 * [((step_rows, d), F32)],
        resident=[(w_out.shape, BF16)],
        scratch=scratch + [carry_in_vmem])
    return pl.pallas_call(
        functools.partial(_sb_out_kernel, n_heads, pool_w),
        out_shape=jax.ShapeDtypeStruct((rows, d), F32),
        grid=(batch, steps),
        in_specs=[row_spec(sb_w), seq_spec, seq_spec, row_spec(sb_w), row_spec(pm.shape[1]),
                  row_spec(d), full(w_out)],
        out_specs=row_spec(d),
        scratch_shapes=[pltpu.VMEM(*s) for s in scratch],
        compiler_params=pltpu.CompilerParams(dimension_semantics=("arbitrary", "arbitrary"),
                                             vmem_limit_bytes=vmem),
        name="sb_out",
    )(q, k3, v3, gate, pm, x2d, w_out)


def kernel(x, mem, norm_g, w_in, pool_w, pool_scale, mem_norm_g, w_mem_kv, q_norm_g, k_norm_g, w_out):
    batch, seq_len, d = x.shape
    mem_len = mem.shape[1]
    depth = norm_g.shape[0]
    pool_width = pool_scale.shape[1]
    assert seq_len % TQ == 0
    x2d = x.reshape(batch * seq_len, d)
    for l in range(depth):
        q, k, v, gate, pm, w_out_b = _in_proj(
            x2d, norm_g[l][None], w_in[l], pool_w[l], pool_scale[l][None],
            mem.reshape(batch * mem_len, d), mem_norm_g[l][None], w_mem_kv[l], k_norm_g[l][None],
            q_norm_g[l][None], w_out[l], seq_len, mem_len)
        x2d = _sb_out(q, k, v, gate, pm, x2d, w_out_b, pool_width, batch, seq_len)
    return x2d.reshape(batch, seq_len, d)
```

```python
import functools

import jax
import jax.numpy as jnp
from jax import lax
from jax.experimental import pallas as pl
from jax.experimental.pallas import tpu as pltpu

F32 = jnp.float32
BF16 = jnp.bfloat16

EPS = 1e-6
HEAD_DIM = 64
POOL_WINDOWS = (2, 4, 8, 16)
MAX_WINDOW = max(POOL_WINDOWS)
MEM_HEADS = 4
LANES = 128
HEADS_PER_LANE_TILE = LANES // HEAD_DIM
PV_GROUP = HEADS_PER_LANE_TILE
PV_WIDTH = PV_GROUP * HEAD_DIM

TM = 512
WEIGHT_CAST_CHUNKS = 8
TM_SUB = 256
TQ = 256
Q_TILES = 4
V7X_VMEM_BYTES = 64 * 1024 * 1024
TEMPORARIES_BYTES = 8 * 1024 * 1024


def _nbytes(shape, dtype):
    n = jnp.dtype(dtype).itemsize
    for s in shape:
        n *= s
    return n


def _vmem_limit(streamed, resident, scratch):
    total = (2 * sum(_nbytes(*b) for b in streamed) + sum(_nbytes(*b) for b in resident)
             + sum(_nbytes(*b) for b in scratch) + TEMPORARIES_BYTES)
    assert total <= V7X_VMEM_BYTES, total
    return total


def _silu(g):
    return g / (1.0 + jnp.exp(-g))


def _dot(a, b):
    return jnp.dot(a, b, preferred_element_type=F32)


def _dot_nt(a, b):
    return lax.dot_general(a, b, (((1,), (1,)), ((), ())), preferred_element_type=F32)


def _per_head(g):
    return jnp.concatenate([g] * MEM_HEADS, axis=1)


def _block_diag(w_ref):
    g, c, _ = w_ref.shape
    zero = jnp.zeros((c, c), F32)
    rows = [jnp.concatenate([w_ref[i] if j == i else zero for j in range(g)], axis=1) for i in range(g)]
    return jnp.concatenate(rows, axis=0).astype(BF16)


def _segment_mean_matrix(width, seg):
    r = lax.broadcasted_iota(jnp.int32, (width, width), 0) // seg
    c = lax.broadcasted_iota(jnp.int32, (width, width), 1) // seg
    return jnp.where(r == c, 1.0 / seg, 0.0).astype(BF16)


DONE, MIX_READY = object(), object()

def _in_proj_kernel(seq_len, pool_w, sb_w, mem_w,
                    x_ref, g_ref, w_in_ref, pw_ref, pscale_ref, mem_ref, mg_ref, w_mem_ref, kg_ref, qg_ref,
                    w_out_ref,
                    q_out, k_out, v_out, gate_out, pm_out, w_out_out,
                    pbuf, halo_ref, w_ref, mk_ref, mv_ref):
    tm = x_ref.shape[0]
    n_sub = pbuf.shape[0]
    rows = tm // n_sub
    t0 = (pl.program_id(0) * tm) % seq_len
    head_w = mem_w // MEM_HEADS

    @pl.when(pl.program_id(0) == 0)
    def _():
        chunk = w_in_ref.shape[0] // WEIGHT_CAST_CHUNKS
        for c in range(WEIGHT_CAST_CHUNKS):
            w_ref[c * chunk:(c + 1) * chunk, :] = w_in_ref[c * chunk:(c + 1) * chunk, :].astype(BF16)
        w_out_out[...] = w_out_ref[...].astype(BF16)

    @pl.when(t0 == 0)
    def _():
        m = mem_ref[...]
        ms = jnp.mean(m * m, axis=-1, keepdims=True)
        hm = (m * lax.rsqrt(ms + EPS) * mg_ref[...]).astype(BF16)
        kv = _dot(hm, w_mem_ref[...].astype(BF16))
        k = kv[:, :mem_w]
        kms = _dot((k * k).astype(BF16), _segment_mean_matrix(mem_w, head_w))
        mk_ref[...] = (k * lax.rsqrt(kms + EPS) * _per_head(kg_ref[...])).astype(BF16)
        mv_ref[...] = kv[:, mem_w:].astype(BF16)

    @pl.when(t0 == 0)
    def _():
        halo_ref[...] = jnp.zeros(halo_ref.shape, F32)

    lane = lax.broadcasted_iota(jnp.int32, (rows, pool_w), 1)
    group = lane // (pool_w // len(POOL_WINDOWS))
    row_iota = lax.broadcasted_iota(jnp.int32, (rows, pool_w), 0)
    seg = _segment_mean_matrix(mem_w, head_w)
    q_gain = _per_head(qg_ref[...])
    pool_mix = _block_diag(pw_ref)
    lane_q = lax.broadcasted_iota(jnp.int32, (rows, mem_w), 1) // head_w
    mv = mv_ref[...].astype(F32)
    lane_v = lax.broadcasted_iota(jnp.int32, mv.shape, 1) // head_w
    mv_heads = [jnp.where(lane_v == hd, mv, 0.0).astype(BF16) for hd in range(MEM_HEADS)]

    h, kept = {}, {}

    def normalize(s, n_chunks=4):
        chunk = rows // n_chunks
        parts = []
        for i in range(n_chunks):
            x = x_ref[s * rows + i * chunk:s * rows + (i + 1) * chunk, :]
            ms = jnp.mean(x * x, axis=-1, keepdims=True)
            parts.append((x * lax.rsqrt(ms + EPS) * g_ref[...]).astype(BF16))
            if i + 1 < n_chunks:
                yield
        h[s] = jnp.concatenate(parts, axis=0)

    def project(s):
        hs = h.pop(s)
        rs = slice(s * rows, (s + 1) * rows)

        def proj(c0, width):
            return _dot(hs, w_ref[:, c0:c0 + width])

        c_pool, c_sb, c_mem = 0, 2 * pool_w, 2 * pool_w + 4 * sb_w
        pool_v = proj(c_pool, pool_w)
        yield
        pool_g = proj(c_pool + pool_w, pool_w)
        yield
        mem_q = proj(c_mem, mem_w)
        yield
        mem_g = proj(c_mem + mem_w, mem_w)
        kept[s] = (pool_v, pool_g, mem_q, mem_g)
        yield MIX_READY
        q_out[rs, :] = (proj(c_sb, sb_w) * HEAD_DIM ** -0.5).astype(BF16)
        yield
        k_out[rs, :] = proj(c_sb + sb_w, sb_w).astype(BF16)
        yield
        v_out[rs, :] = proj(c_sb + 2 * sb_w, sb_w).astype(BF16)
        yield
        gate_out[rs, :] = _silu(proj(c_sb + 3 * sb_w, sb_w)).astype(BF16)

    def mix(s):
        pool_v, pool_g, mem_q, mem_g = kept.pop(s)
        rs = slice(s * rows, (s + 1) * rows)

        pbuf[s, 0:MAX_WINDOW, :] = halo_ref[...]
        pbuf[s, MAX_WINDOW:MAX_WINDOW + rows, :] = pool_v
        halo_ref[...] = pool_v[rows - MAX_WINDOW:, :]
        t_seq = t0 + s * rows + row_iota
        win_sum = pool_v
        window = jnp.full((rows, pool_w), 1, jnp.int32)
        acc = pool_v
        shift = 1
        for g, w in enumerate(POOL_WINDOWS):
            while shift < w:
                acc = acc + pbuf[s, MAX_WINDOW - shift:MAX_WINDOW - shift + rows, :]
                shift += 1
            win_sum = jnp.where(group == g, acc, win_sum)
            window = jnp.where(group == g, w, window)
        count = jnp.minimum(t_seq + 1, window).astype(F32)
        pooled = (win_sum / count - pool_v).astype(BF16)
        y_pool = _dot(pooled, pool_mix)
        yield

        qms = _dot((mem_q * mem_q).astype(BF16), seg)
        yield
        pm_out[rs, 0:pool_w] = (y_pool * pscale_ref[...] * _silu(pool_g)).astype(BF16)
        qn = mem_q * lax.rsqrt(qms + EPS) * q_gain
        qn = qn * (head_w ** -0.5)

        scores = []
        for hd in range(MEM_HEADS):
            scores.append(_dot_nt(jnp.where(lane_q == hd, qn, 0.0).astype(BF16), mk_ref[...]))
            yield
        y_mem = None
        for hd in range(MEM_HEADS):
            e = jnp.exp(scores[hd] - jnp.max(scores[hd], axis=-1, keepdims=True))
            pv = _dot(e.astype(BF16), mv_heads[hd])
            yield
            pv = pv * (1.0 / jnp.sum(e, axis=-1, keepdims=True))
            y_mem = pv if y_mem is None else y_mem + pv
        pm_out[rs, pool_w:pool_w + mem_w] = (y_mem * _silu(mem_g)).astype(BF16)

    def sub_tile(s):
        yield from normalize(s)
        yield from project(s)

    stages = [(0, sub_tile(0))]
    while stages:
        for entry in list(stages):
            s, stage = entry
            event = next(stage, DONE)
            if event is DONE:
                stages.remove(entry)
            elif event is MIX_READY:
                stages.append((s, mix(s)))
                if s + 1 < n_sub:
                    stages.append((s + 1, sub_tile(s + 1)))


def _in_proj(x2d, norm_g, w_in, pool_w3, pool_scale, mem2d, mem_norm_g, w_mem_kv, k_norm_g, q_norm_g, w_out,
             seq_len, mem_len):
    rows, d = x2d.shape
    pool_w = pool_w3.shape[0] * pool_w3.shape[1]
    mem_w = w_mem_kv.shape[1] // 2
    sb_w = (w_in.shape[1] - 2 * pool_w - 2 * mem_w) // 4
    assert seq_len % TM == 0 and rows % TM == 0
    tiles_per_seq = seq_len // TM
    row_spec = lambda width: pl.BlockSpec((TM, width), lambda i: (i, 0))
    full = lambda a: pl.BlockSpec(a.shape, lambda i: (0,) * a.ndim)
    mem_spec = pl.BlockSpec((mem_len, d), lambda i: (i // tiles_per_seq, 0))
    out_sd = lambda width: jax.ShapeDtypeStruct((rows, width), BF16)
    scratch = [((TM // TM_SUB, MAX_WINDOW + TM_SUB, pool_w), F32), ((MAX_WINDOW, pool_w), F32),
               (w_in.shape, BF16), ((mem_len, mem_w), BF16), ((mem_len, mem_w), BF16)]
    vmem = _vmem_limit(
        streamed=[((TM, d), F32)] + 4 * [((TM, sb_w), BF16)] + [((TM, pool_w + mem_w), BF16)]
        + [((mem_len, d), F32)],
        resident=[(w_in.shape, F32), (w_out.shape, F32), (w_out.shape, BF16), (w_mem_kv.shape, F32),
                  (pool_w3.shape, F32), (norm_g.shape, F32), (pool_scale.shape, F32),
                  (mem_norm_g.shape, F32), (k_norm_g.shape, F32), (q_norm_g.shape, F32)],
        scratch=scratch)
    return pl.pallas_call(
        functools.partial(_in_proj_kernel, seq_len, pool_w, sb_w, mem_w),
        out_shape=(out_sd(sb_w), out_sd(sb_w), out_sd(sb_w), out_sd(sb_w), out_sd(pool_w + mem_w),
                   jax.ShapeDtypeStruct(w_out.shape, BF16)),
        grid=(rows // TM,),
        in_specs=[row_spec(d), full(norm_g), full(w_in), full(pool_w3), full(pool_scale),
                  mem_spec, full(mem_norm_g), full(w_mem_kv), full(k_norm_g), full(q_norm_g), full(w_out)],
        out_specs=(row_spec(sb_w), row_spec(sb_w), row_spec(sb_w), row_spec(sb_w),
                   row_spec(pool_w + mem_w), full(w_out)),
        scratch_shapes=[pltpu.VMEM(*s) for s in scratch],
        compiler_params=pltpu.CompilerParams(dimension_semantics=("arbitrary",),
                                             vmem_limit_bytes=vmem),
        name="in_proj",
    )(x2d, norm_g, w_in, pool_w3, pool_scale, mem2d, mem_norm_g, w_mem_kv, k_norm_g, q_norm_g, w_out)


def _sb_out_kernel(n_heads, pool_w,
                   q_ref, k_ref, v_ref, gate_ref, pm_ref, x_ref, w_ref,
                   o_ref, tri_ref, qm_ref, vm_ref, acc_ref, carry_ref, mixed_ref):
    tq = tri_ref.shape[0]
    n_tiles = q_ref.shape[0] // tq
    n_pairs = n_heads // HEADS_PER_LANE_TILE
    first_block = pl.program_id(1) * n_tiles
    half = lax.broadcasted_iota(jnp.int32, (1, LANES), 1) // HEAD_DIM

    @pl.when(pl.program_id(1) == 0)
    def _():
        lane_head = (lax.broadcasted_iota(jnp.int32, (tq, v_ref.shape[1]), 1) // HEAD_DIM) % PV_GROUP
        for kb in range(v_ref.shape[0] // tq):
            v = v_ref[kb * tq:(kb + 1) * tq, :].astype(F32)
            for g in range(PV_GROUP):
                vm_ref[kb, g * tq:(g + 1) * tq, :] = jnp.where(lane_head == g, v, 0.0).astype(BF16)

    for t in range(n_tiles):
        for p in range(n_pairs):
            qp = q_ref[t * tq:(t + 1) * tq, p * LANES:(p + 1) * LANES].astype(F32)
            for hh in range(HEADS_PER_LANE_TILE):
                qm_ref[t, p * HEADS_PER_LANE_TILE + hh] = jnp.where(half == hh, qp, 0.0).astype(BF16)

    row = lax.broadcasted_iota(jnp.int32, (tq, tq), 0)
    col = lax.broadcasted_iota(jnp.int32, (tq, tq), 1)
    causal = col < row
    tri_ref[...] = jnp.where(row >= col, -1.0, 0.0).astype(BF16)

    def emit(units):
        logit, suffix, a = {}, {}, {}

        def logits(u):
            t, hd, kb, _ = units[u]
            p = hd // HEADS_PER_LANE_TILE
            k0 = pl.multiple_of(kb * tq, tq)
            z = _dot_nt(qm_ref[t, hd], k_ref[pl.ds(k0, tq), p * LANES:(p + 1) * LANES])
            logit[u] = z.astype(BF16)

        def suffix_sum(u):
            diagonal = units[u][3]
            z = logit[u]
            sp = jnp.maximum(z, 0.0) + jnp.log(1.0 + jnp.exp(-jnp.abs(z)))
            if diagonal:
                sp = jnp.where(causal, sp, 0.0)
            suffix[u] = _dot(sp, tri_ref[...])

        def weights(u):
            t, hd, _, diagonal = units[u]
            log_a = logit.pop(u).astype(F32) + suffix[u]
            if not diagonal:
                log_a = log_a + carry_ref[t, hd]
            w = jnp.exp(log_a)
            if diagonal:
                w = jnp.where(causal, w, 0.0)
                carry_ref[t, hd] = suffix.pop(u)[:, 0:1]
            else:
                carry_ref[t, hd] = carry_ref[t, hd] + suffix.pop(u)[:, 0:1]
            a[u] = w.astype(BF16)

        def weighted_values(u_last):
            t, hd, kb, diagonal = units[u_last]
            grp = hd // PV_GROUP
            a_cat = jnp.concatenate([a.pop(u) for u in range(u_last - PV_GROUP + 1, u_last + 1)], axis=1)
            out = _dot(a_cat, vm_ref[kb, :, grp * PV_WIDTH:(grp + 1) * PV_WIDTH])
            if diagonal:
                acc_ref[t, grp] = out
            else:
                acc_ref[t, grp] = acc_ref[t, grp] + out

        n = len(units)
        for step in range(n + 2):
            if step < n:
                logits(step)
            if 0 <= step - 1 < n:
                suffix_sum(step - 1)
            if 0 <= step - 2 < n:
                weights(step - 2)
                if units[step - 2][1] % PV_GROUP == PV_GROUP - 1:
                    weighted_values(step - 2)

    def head_units(t, kb, diagonal):
        return [(t, hd, kb, diagonal) for hd in range(n_heads)]

    units = []
    for depth in range(n_tiles):
        for t in range(depth, n_tiles):
            units += head_units(t, first_block + t - depth, depth == 0)
    emit(units)

    def body(j, _):
        kb = first_block - 1 - j
        units = []
        for grp in range(n_heads // PV_GROUP):
            for t in range(n_tiles):
                units += [(t, grp * PV_GROUP + g, kb, False) for g in range(PV_GROUP)]
        emit(units)
        return 0

    lax.fori_loop(0, first_block, body, 0)

    for t in range(n_tiles):
        rs = slice(t * tq, (t + 1) * tq)
        for grp in range(n_heads // PV_GROUP):
            sl = slice(grp * PV_WIDTH, (grp + 1) * PV_WIDTH)
            mixed_ref[rs, sl] = (acc_ref[t, grp] * gate_ref[rs, sl].astype(F32)).astype(BF16)
    sb_w = mixed_ref.shape[1]
    o_ref[...] = (x_ref[...] + _dot(mixed_ref[...], w_ref[pool_w:pool_w + sb_w, :])
                  + _dot(pm_ref[:, :pool_w], w_ref[:pool_w, :])
                  + _dot(pm_ref[:, pool_w:], w_ref[pool_w + sb_w:, :]))


def _sb_out(q, k, v, gate, pm, x2d, w_out, pool_w, batch, seq_len):
    rows, sb_w = q.shape
    d = x2d.shape[1]
    n_heads = sb_w // HEAD_DIM
    step_rows = TQ * Q_TILES
    assert seq_len % step_rows == 0
    steps = seq_len // step_rows
    k3 = k.reshape(batch, seq_len, sb_w)
    v3 = v.reshape(batch, seq_len, sb_w)
    row_spec = lambda width: pl.BlockSpec((step_rows, width), lambda b, i: (b * steps + i, 0))
    seq_spec = pl.BlockSpec((None, seq_len, sb_w), lambda b, i: (b, 0, 0))
    full = lambda a: pl.BlockSpec(a.shape, lambda b, i: (0, 0))
    scratch = [((TQ, TQ), BF16),
               ((Q_TILES, n_heads, TQ, LANES), BF16),
               ((seq_len // TQ, PV_GROUP * TQ, sb_w), BF16),
               ((Q_TILES, n_heads // PV_GROUP, TQ, PV_WIDTH), F32),
               ((Q_TILES, n_heads, TQ, 1), F32),
               ((step_rows, sb_w), BF16)]
    carry_in_vmem = ((Q_TILES, n_heads, TQ, LANES), F32)
    vmem = _vmem_limit(
        streamed=2 * [((step_rows, sb_w), BF16)] + [((step_rows, pm.shape[1]), BF16)]
        + 2 * [((seq_len, sb_w), BF16)] + 2 * [((step_rows, d), F32)],
        resident=[(w_out.shape, BF16)],
        scratch=scratch + [carry_in_vmem])
    return pl.pallas_call(
        functools.partial(_sb_out_kernel, n_heads, pool_w),
        out_shape=jax.ShapeDtypeStruct((rows, d), F32),
        grid=(batch, steps),
        in_specs=[row_spec(sb_w), seq_spec, seq_spec, row_spec(sb_w), row_spec(pm.shape[1]),
                  row_spec(d), full(w_out)],
        out_specs=row_spec(d),
        scratch_shapes=[pltpu.VMEM(*s) for s in scratch],
        compiler_params=pltpu.CompilerParams(dimension_semantics=("arbitrary", "arbitrary"),
                                             vmem_limit_bytes=vmem),
        name="sb_out",
    )(q, k3, v3, gate, pm, x2d, w_out)


def kernel(x, mem, norm_g, w_in, pool_w, pool_scale, mem_norm_g, w_mem_kv, q_norm_g, k_norm_g, w_out):
    batch, seq_len, d = x.shape
    mem_len = mem.shape[1]
    depth = norm_g.shape[0]
    pool_width = pool_scale.shape[1]
    assert seq_len % TQ == 0
    x2d = x.reshape(batch * seq_len, d)
    for l in range(depth):
        q, k, v, gate, pm, w_out_b = _in_proj(
            x2d, norm_g[l][None], w_in[l], pool_w[l], pool_scale[l][None],
            mem.reshape(batch * mem_len, d), mem_norm_g[l][None], w_mem_kv[l], k_norm_g[l][None],
            q_norm_g[l][None], w_out[l], seq_len, mem_len)
        x2d = _sb_out(q, k, v, gate, pm, x2d, w_out_b, pool_width, batch, seq_len)
    return x2d.reshape(batch, seq_len, d)
```

```python
import functools

import jax
import jax.numpy as jnp
from jax import lax
from jax.experimental import pallas as pl
from jax.experimental.pallas import tpu as pltpu

F32 = jnp.float32
BF16 = jnp.bfloat16

EPS = 1e-6
HEAD_DIM = 64
POOL_WINDOWS = (2, 4, 8, 16)
MAX_WINDOW = max(POOL_WINDOWS)
MEM_HEADS = 4
LANES = 128
HEADS_PER_LANE_TILE = LANES // HEAD_DIM
PV_GROUP = HEADS_PER_LANE_TILE
PV_WIDTH = PV_GROUP * HEAD_DIM

TM = 512
WEIGHT_CAST_CHUNKS = 8
TM_SUB = 256
TQ = 256
Q_TILES = 4
V7X_VMEM_BYTES = 64 * 1024 * 1024
TEMPORARIES_BYTES = 8 * 1024 * 1024


def _nbytes(shape, dtype):
    n = jnp.dtype(dtype).itemsize
    for s in shape:
        n *= s
    return n


def _vmem_limit(streamed, resident, scratch):
    total = (2 * sum(_nbytes(*b) for b in streamed) + sum(_nbytes(*b) for b in resident)
             + sum(_nbytes(*b) for b in scratch) + TEMPORARIES_BYTES)
    assert total <= V7X_VMEM_BYTES, total
    return total


def _silu(g):
    return g / (1.0 + jnp.exp(-g))


def _dot(a, b):
    return jnp.dot(a, b, preferred_element_type=F32)


def _dot_nt(a, b):
    return lax.dot_general(a, b, (((1,), (1,)), ((), ())), preferred_element_type=F32)


def _per_head(g):
    return jnp.concatenate([g] * MEM_HEADS, axis=1)


def _block_diag(w_ref):
    g, c, _ = w_ref.shape
    zero = jnp.zeros((c, c), F32)
    rows = [jnp.concatenate([w_ref[i] if j == i else zero for j in range(g)], axis=1) for i in range(g)]
    return jnp.concatenate(rows, axis=0).astype(BF16)


def _segment_mean_matrix(width, seg):
    r = lax.broadcasted_iota(jnp.int32, (width, width), 0) // seg
    c = lax.broadcasted_iota(jnp.int32, (width, width), 1) // seg
    return jnp.where(r == c, 1.0 / seg, 0.0).astype(BF16)


DONE, MIX_READY = object(), object()

def _in_proj_kernel(seq_len, pool_w, sb_w, mem_w,
                    x_ref, g_ref, w_in_ref, pw_ref, pscale_ref, mem_ref, mg_ref, w_mem_ref, kg_ref, qg_ref,
                    w_out_ref,
                    q_out, k_out, v_out, gate_out, pm_out, w_out_out,
                    pbuf, halo_ref, w_ref, mk_ref, mv_ref):
    tm = x_ref.shape[0]
    n_sub = pbuf.shape[0]
    rows = tm // n_sub
    t0 = (pl.program_id(0) * tm) % seq_len
    head_w = mem_w // MEM_HEADS

    @pl.when(pl.program_id(0) == 0)
    def _():
        chunk = w_in_ref.shape[0] // WEIGHT_CAST_CHUNKS
        for c in range(WEIGHT_CAST_CHUNKS):
            w_ref[c * chunk:(c + 1) * chunk, :] = w_in_ref[c * chunk:(c + 1) * chunk, :].astype(BF16)
        w_out_out[...] = w_out_ref[...].astype(BF16)

    @pl.when(t0 == 0)
    def _():
        m = mem_ref[...]
        ms = jnp.mean(m * m, axis=-1, keepdims=True)
        hm = (m * lax.rsqrt(ms + EPS) * mg_ref[...]).astype(BF16)
        kv = _dot(hm, w_mem_ref[...].astype(BF16))
        k = kv[:, :mem_w]
        kms = _dot((k * k).astype(BF16), _segment_mean_matrix(mem_w, head_w))
        mk_ref[...] = (k * lax.rsqrt(kms + EPS) * _per_head(kg_ref[...])).astype(BF16)
        mv_ref[...] = kv[:, mem_w:].astype(BF16)

    @pl.when(t0 == 0)
    def _():
        halo_ref[...] = jnp.zeros(halo_ref.shape, F32)

    lane = lax.broadcasted_iota(jnp.int32, (rows, pool_w), 1)
    group = lane // (pool_w // len(POOL_WINDOWS))
    row_iota = lax.broadcasted_iota(jnp.int32, (rows, pool_w), 0)
    seg = _segment_mean_matrix(mem_w, head_w)
    q_gain = _per_head(qg_ref[...])
    pool_mix = _block_diag(pw_ref)
    lane_q = lax.broadcasted_iota(jnp.int32, (rows, mem_w), 1) // head_w
    mv = mv_ref[...].astype(F32)
    lane_v = lax.broadcasted_iota(jnp.int32, mv.shape, 1) // head_w
    mv_heads = [jnp.where(lane_v == hd, mv, 0.0).astype(BF16) for hd in range(MEM_HEADS)]

    h, kept = {}, {}

    def normalize(s, n_chunks=4):
        chunk = rows // n_chunks
        parts = []
        for i in range(n_chunks):
            x = x_ref[s * rows + i * chunk:s * rows + (i + 1) * chunk, :]
            ms = jnp.mean(x * x, axis=-1, keepdims=True)
            parts.append((x * lax.rsqrt(ms + EPS) * g_ref[...]).astype(BF16))
            if i + 1 < n_chunks:
                yield
        h[s] = jnp.concatenate(parts, axis=0)

    def project(s):
        hs = h.pop(s)
        rs = slice(s * rows, (s + 1) * rows)

        def proj(c0, width):
            return _dot(hs, w_ref[:, c0:c0 + width])

        c_pool, c_sb, c_mem = 0, 2 * pool_w, 2 * pool_w + 4 * sb_w
        pool_v = proj(c_pool, pool_w)
        yield
        pool_g = proj(c_pool + pool_w, pool_w)
        yield
        mem_q = proj(c_mem, mem_w)
        yield
        mem_g = proj(c_mem + mem_w, mem_w)
        kept[s] = (pool_v, pool_g, mem_q, mem_g)
        yield MIX_READY
        q_out[rs, :] = (proj(c_sb, sb_w) * HEAD_DIM ** -0.5).astype(BF16)
        yield
        k_out[rs, :] = proj(c_sb + sb_w, sb_w).astype(BF16)
        yield
        v_out[rs, :] = proj(c_sb + 2 * sb_w, sb_w).astype(BF16)
        yield
        gate_out[rs, :] = _silu(proj(c_sb + 3 * sb_w, sb_w)).astype(BF16)

    def mix(s):
        pool_v, pool_g, mem_q, mem_g = kept.pop(s)
        rs = slice(s * rows, (s + 1) * rows)

        pbuf[s, 0:MAX_WINDOW, :] = halo_ref[...]
        pbuf[s, MAX_WINDOW:MAX_WINDOW + rows, :] = pool_v
        halo_ref[...] = pool_v[rows - MAX_WINDOW:, :]
        t_seq = t0 + s * rows + row_iota
        win_sum = pool_v
        window = jnp.full((rows, pool_w), 1, jnp.int32)
        acc = pool_v
        shift = 1
        for g, w in enumerate(POOL_WINDOWS):
            while shift < w:
                acc = acc + pbuf[s, MAX_WINDOW - shift:MAX_WINDOW - shift + rows, :]
                shift += 1
            win_sum = jnp.where(group == g, acc, win_sum)
            window = jnp.where(group == g, w, window)
        count = jnp.minimum(t_seq + 1, window).astype(F32)
        pooled = (win_sum / count - pool_v).astype(BF16)
        y_pool = _dot(pooled, pool_mix)
        yield

        sq = mem_q * mem_q
        qms = jnp.concatenate(
            [jnp.broadcast_to(jnp.mean(sq[:, hd * head_w:(hd + 1) * head_w], axis=-1, keepdims=True),
                              (rows, head_w)) for hd in range(MEM_HEADS)], axis=1)
        pm_out[rs, 0:pool_w] = (y_pool * pscale_ref[...] * _silu(pool_g)).astype(BF16)
        qn = mem_q * lax.rsqrt(qms + EPS) * q_gain
        qn = qn * (head_w ** -0.5)

        scores = []
        for hd in range(MEM_HEADS):
            scores.append(_dot_nt(jnp.where(lane_q == hd, qn, 0.0).astype(BF16), mk_ref[...]))
            yield
        y_mem = None
        for hd in range(MEM_HEADS):
            e = jnp.exp(scores[hd] - jnp.max(scores[hd], axis=-1, keepdims=True))
            pv = _dot(e.astype(BF16), mv_heads[hd])
            yield
            pv = pv * (1.0 / jnp.sum(e, axis=-1, keepdims=True))
            y_mem = pv if y_mem is None else y_mem + pv
        pm_out[rs, pool_w:pool_w + mem_w] = (y_mem * _silu(mem_g)).astype(BF16)

    def sub_tile(s):
        yield from normalize(s)
        yield from project(s)

    stages = [(0, sub_tile(0))]
    while stages:
        for entry in list(stages):
            s, stage = entry
            event = next(stage, DONE)
            if event is DONE:
                stages.remove(entry)
            elif event is MIX_READY:
                stages.append((s, mix(s)))
                if s + 1 < n_sub:
                    stages.append((s + 1, sub_tile(s + 1)))


def _in_proj(x2d, norm_g, w_in, pool_w3, pool_scale, mem2d, mem_norm_g, w_mem_kv, k_norm_g, q_norm_g, w_out,
             seq_len, mem_len):
    rows, d = x2d.shape
    pool_w = pool_w3.shape[0] * pool_w3.shape[1]
    mem_w = w_mem_kv.shape[1] // 2
    sb_w = (w_in.shape[1] - 2 * pool_w - 2 * mem_w) // 4
    assert seq_len % TM == 0 and rows % TM == 0
    tiles_per_seq = seq_len // TM
    row_spec = lambda width: pl.BlockSpec((TM, width), lambda i: (i, 0))
    full = lambda a: pl.BlockSpec(a.shape, lambda i: (0,) * a.ndim)
    mem_spec = pl.BlockSpec((mem_len, d), lambda i: (i // tiles_per_seq, 0))
    out_sd = lambda width: jax.ShapeDtypeStruct((rows, width), BF16)
    scratch = [((TM // TM_SUB, MAX_WINDOW + TM_SUB, pool_w), F32), ((MAX_WINDOW, pool_w), F32),
               (w_in.shape, BF16), ((mem_len, mem_w), BF16), ((mem_len, mem_w), BF16)]
    vmem = _vmem_limit(
        streamed=[((TM, d), F32)] + 4 * [((TM, sb_w), BF16)] + [((TM, pool_w + mem_w), BF16)]
        + [((mem_len, d), F32)],
        resident=[(w_in.shape, F32), (w_out.shape, F32), (w_out.shape, BF16), (w_mem_kv.shape, F32),
                  (pool_w3.shape, F32), (norm_g.shape, F32), (pool_scale.shape, F32),
                  (mem_norm_g.shape, F32), (k_norm_g.shape, F32), (q_norm_g.shape, F32)],
        scratch=scratch)
    return pl.pallas_call(
        functools.partial(_in_proj_kernel, seq_len, pool_w, sb_w, mem_w),
        out_shape=(out_sd(sb_w), out_sd(sb_w), out_sd(sb_w), out_sd(sb_w), out_sd(pool_w + mem_w),
                   jax.ShapeDtypeStruct(w_out.shape, BF16)),
        grid=(rows // TM,),
        in_specs=[row_spec(d), full(norm_g), full(w_in), full(pool_w3), full(pool_scale),
                  mem_spec, full(mem_norm_g), full(w_mem_kv), full(k_norm_g), full(q_norm_g), full(w_out)],
        out_specs=(row_spec(sb_w), row_spec(sb_w), row_spec(sb_w), row_spec(sb_w),
                   row_spec(pool_w + mem_w), full(w_out)),
        scratch_shapes=[pltpu.VMEM(*s) for s in scratch],
        compiler_params=pltpu.CompilerParams(dimension_semantics=("arbitrary",),
                                             vmem_limit_bytes=vmem),
        name="in_proj",
    )(x2d, norm_g, w_in, pool_w3, pool_scale, mem2d, mem_norm_g, w_mem_kv, k_norm_g, q_norm_g, w_out)


def _sb_out_kernel(n_heads, pool_w,
                   q_ref, k_ref, v_ref, gate_ref, pm_ref, x_ref, w_ref,
                   o_ref, tri_ref, qm_ref, vm_ref, acc_ref, carry_ref, mixed_ref):
    tq = tri_ref.shape[0]
    n_tiles = q_ref.shape[0] // tq
    n_pairs = n_heads // HEADS_PER_LANE_TILE
    first_block = pl.program_id(1) * n_tiles
    half = lax.broadcasted_iota(jnp.int32, (1, LANES), 1) // HEAD_DIM

    @pl.when(pl.program_id(1) == 0)
    def _():
        lane_head = (lax.broadcasted_iota(jnp.int32, (tq, v_ref.shape[1]), 1) // HEAD_DIM) % PV_GROUP
        for kb in range(v_ref.shape[0] // tq):
            v = v_ref[kb * tq:(kb + 1) * tq, :].astype(F32)
            for g in range(PV_GROUP):
                vm_ref[kb, g * tq:(g + 1) * tq, :] = jnp.where(lane_head == g, v, 0.0).astype(BF16)

    for t in range(n_tiles):
        for p in range(n_pairs):
            qp = q_ref[t * tq:(t + 1) * tq, p * LANES:(p + 1) * LANES].astype(F32)
            for hh in range(HEADS_PER_LANE_TILE):
                qm_ref[t, p * HEADS_PER_LANE_TILE + hh] = jnp.where(half == hh, qp, 0.0).astype(BF16)

    row = lax.broadcasted_iota(jnp.int32, (tq, tq), 0)
    col = lax.broadcasted_iota(jnp.int32, (tq, tq), 1)
    causal = col < row
    tri_ref[...] = jnp.where(row >= col, -1.0, 0.0).astype(BF16)

    def emit(units):
        logit, suffix, a = {}, {}, {}

        def logits(u):
            t, hd, kb, _ = units[u]
            p = hd // HEADS_PER_LANE_TILE
            k0 = pl.multiple_of(kb * tq, tq)
            z = _dot_nt(qm_ref[t, hd], k_ref[pl.ds(k0, tq), p * LANES:(p + 1) * LANES])
            logit[u] = z.astype(BF16)

        def suffix_sum(u):
            diagonal = units[u][3]
            z = logit[u]
            sp = jnp.maximum(z, 0.0) + jnp.log(1.0 + jnp.exp(-jnp.abs(z)))
            if diagonal:
                sp = jnp.where(causal, sp, 0.0)
            suffix[u] = _dot(sp, tri_ref[...])

        def weights(u):
            t, hd, _, diagonal = units[u]
            log_a = logit.pop(u).astype(F32) + suffix[u]
            if not diagonal:
                log_a = log_a + carry_ref[t, hd]
            w = jnp.exp(log_a)
            if diagonal:
                w = jnp.where(causal, w, 0.0)
                carry_ref[t, hd] = suffix.pop(u)[:, 0:1]
            else:
                carry_ref[t, hd] = carry_ref[t, hd] + suffix.pop(u)[:, 0:1]
            a[u] = w.astype(BF16)

        def weighted_values(u_last):
            t, hd, kb, diagonal = units[u_last]
            grp = hd // PV_GROUP
            a_cat = jnp.concatenate([a.pop(u) for u in range(u_last - PV_GROUP + 1, u_last + 1)], axis=1)
            out = _dot(a_cat, vm_ref[kb, :, grp * PV_WIDTH:(grp + 1) * PV_WIDTH])
            if diagonal:
                acc_ref[t, grp] = out
            else:
                acc_ref[t, grp] = acc_ref[t, grp] + out

        n = len(units)
        for step in range(n + 2):
            if step < n:
                logits(step)
            if 0 <= step - 1 < n:
                suffix_sum(step - 1)
            if 0 <= step - 2 < n:
                weights(step - 2)
                if units[step - 2][1] % PV_GROUP == PV_GROUP - 1:
                    weighted_values(step - 2)

    def head_units(t, kb, diagonal):
        return [(t, hd, kb, diagonal) for hd in range(n_heads)]

    units = []
    for depth in range(n_tiles):
        for t in range(depth, n_tiles):
            units += head_units(t, first_block + t - depth, depth == 0)
    emit(units)

    def body(j, _):
        kb = first_block - 1 - j
        units = []
        for grp in range(n_heads // PV_GROUP):
            for t in range(n_tiles):
                units += [(t, grp * PV_GROUP + g, kb, False) for g in range(PV_GROUP)]
        emit(units)
        return 0

    lax.fori_loop(0, first_block, body, 0)

    for t in range(n_tiles):
        rs = slice(t * tq, (t + 1) * tq)
        for grp in range(n_heads // PV_GROUP):
            sl = slice(grp * PV_WIDTH, (grp + 1) * PV_WIDTH)
            mixed_ref[rs, sl] = (acc_ref[t, grp] * gate_ref[rs, sl].astype(F32)).astype(BF16)
    sb_w = mixed_ref.shape[1]
    o_ref[...] = (x_ref[...] + _dot(mixed_ref[...], w_ref[pool_w:pool_w + sb_w, :])
                  + _dot(pm_ref[:, :pool_w], w_ref[:pool_w, :])
                  + _dot(pm_ref[:, pool_w:], w_ref[pool_w + sb_w:, :]))


def _sb_out(q, k, v, gate, pm, x2d, w_out, pool_w, batch, seq_len):
    rows, sb_w = q.shape
    d = x2d.shape[1]
    n_heads = sb_w // HEAD_DIM
    step_rows = TQ * Q_TILES
    assert seq_len % step_rows == 0
    steps = seq_len // step_rows
    k3 = k.reshape(batch, seq_len, sb_w)
    v3 = v.reshape(batch, seq_len, sb_w)
    row_spec = lambda width: pl.BlockSpec((step_rows, width), lambda b, i: (b * steps + i, 0))
    seq_spec = pl.BlockSpec((None, seq_len, sb_w), lambda b, i: (b, 0, 0))
    full = lambda a: pl.BlockSpec(a.shape, lambda b, i: (0, 0))
    scratch = [((TQ, TQ), BF16),
               ((Q_TILES, n_heads, TQ, LANES), BF16),
               ((seq_len // TQ, PV_GROUP * TQ, sb_w), BF16),
               ((Q_TILES, n_heads // PV_GROUP, TQ, PV_WIDTH), F32),
               ((Q_TILES, n_heads, TQ, 1), F32),
               ((step_rows, sb_w), BF16)]
    carry_in_vmem = ((Q_TILES, n_heads, TQ, LANES), F32)
    vmem = _vmem_limit(
        streamed=2 * [((step_rows, sb_w), BF16)] + [((step_rows, pm.shape[1]), BF16)]
        + 2 * [((seq_len, sb_w), BF16)] + 2 * [((step_rows, d), F32)],
        resident=[(w_out.shape, BF16)],
        scratch=scratch + [carry_in_vmem])
    return pl.pallas_call(
        functools.partial(_sb_out_kernel, n_heads, pool_w),
        out_shape=jax.ShapeDtypeStruct((rows, d), F32),
        grid=(batch, steps),
        in_specs=[row_spec(sb_w), seq_spec, seq_spec, row_spec(sb_w), row_spec(pm.shape[1]),
                  row_spec(d), full(w_out)],
        out_specs=row_spec(d),
        scratch_shapes=[pltpu.VMEM(*s) for s in scratch],
        compiler_params=pltpu.CompilerParams(dimension_semantics=("arbitrary", "arbitrary"),
                                             vmem_limit_bytes=vmem),
        name="sb_out",
    )(q, k3, v3, gate, pm, x2d, w_out)


def kernel(x, mem, norm_g, w_in, pool_w, pool_scale, mem_norm_g, w_mem_kv, q_norm_g, k_norm_g, w_out):
    batch, seq_len, d = x.shape
    mem_len = mem.shape[1]
    depth = norm_g.shape[0]
    pool_width = pool_scale.shape[1]
    assert seq_len % TQ == 0
    x2d = x.reshape(batch * seq_len, d)
    for l in range(depth):
        q, k, v, gate, pm, w_out_b = _in_proj(
            x2d, norm_g[l][None], w_in[l], pool_w[l], pool_scale[l][None],
            mem.reshape(batch * mem_len, d), mem_norm_g[l][None], w_mem_kv[l], k_norm_g[l][None],
            q_norm_g[l][None], w_out[l], seq_len, mem_len)
        x2d = _sb_out(q, k, v, gate, pm, x2d, w_out_b, pool_width, batch, seq_len)
    return x2d.reshape(batch, seq_len, d)
```
